```python
import math
import jax, jax.numpy as jnp
from jax import lax
import numpy as np

D_MODEL = 1024
BATCH = 8
SEQ = 4096
DEPTH = 2

D_MIX = D_MODEL
A_WIDTH = D_MIX // 2
A_GROUPS = 8
A_GDIM = A_WIDTH // A_GROUPS
CHUNK = 128
B_HEADS = 8
HEAD_DIM = (D_MIX - A_WIDTH) // B_HEADS
B_WIDTH = B_HEADS * HEAD_DIM
IDX_HEADS = 8
IDX_DIM = HEAD_DIM
TOPK_MAX = 256
Q_BLOCK = 128
ROPE_THETA = 10000.0
LN_EPS = 1e-5
ALPHA = (2.0 * DEPTH) ** 0.25
BETA = (8.0 * DEPTH) ** -0.25

SPLITS = (A_WIDTH, A_WIDTH, A_WIDTH, B_WIDTH, HEAD_DIM, HEAD_DIM, B_WIDTH,
          IDX_HEADS * IDX_DIM, IDX_DIM, IDX_HEADS)
D_IN = sum(SPLITS)
SPLIT_POINTS = tuple(int(p) for p in np.cumsum(SPLITS)[:-1])

kernel_name = "hymba_gmlp_dsa_deepnorm_adaln"


def layer_norm(x, g=None, b=None):
    xf = x.astype(jnp.float32)
    mu = jnp.mean(xf, axis=-1, keepdims=True)
    var = jnp.mean(jnp.square(xf - mu), axis=-1, keepdims=True)
    y = (xf - mu) * lax.rsqrt(var + LN_EPS)
    if g is not None:
        y = y * g.astype(jnp.float32) + b.astype(jnp.float32)
    return y.astype(x.dtype)


def rope_tables(positions):
    inv_freq = ROPE_THETA ** (-jnp.arange(0, HEAD_DIM, 2, dtype=jnp.float32) / HEAD_DIM)
    ang = positions.astype(jnp.float32)[..., None] * inv_freq
    return jnp.cos(ang), jnp.sin(ang)


def apply_rope(x, cos, sin):
    x1, x2 = jnp.split(x.astype(jnp.float32), 2, axis=-1)
    return jnp.concatenate([x1 * cos - x2 * sin, x2 * cos + x1 * sin], axis=-1).astype(x.dtype)


def spatial_gating_unit(u, v, g_v, b_v, w_s, b_s):
    bn, s, _ = v.shape
    vn = layer_norm(v, g_v, b_v).reshape(bn, s // CHUNK, CHUNK, A_GROUPS, A_GDIM)
    causal = jnp.tril(jnp.ones((CHUNK, CHUNK), dtype=bool))
    w_m = jnp.where(causal[None], w_s, jnp.zeros_like(w_s))
    mixed = jnp.einsum('gts,bcsgd->bctgd', w_m, vn) + b_s.T[:, :, None]
    return u * mixed.reshape(bn, s, A_WIDTH)


def sparse_attention(q, k, v, q_idx, k_idx, w_idx):
    bn, s, _, _ = q.shape
    top = min(TOPK_MAX, s // 4)
    n_blk = s // Q_BLOCK
    key_pos = jnp.arange(s)
    attn_scale = HEAD_DIM ** -0.5
    w_scaled = w_idx * (IDX_HEADS ** -0.5)
    gather = jax.vmap(lambda tab, idx: tab[idx])

    def block(i):
        start = i * Q_BLOCK
        qb = lax.dynamic_slice_in_dim(q, start, Q_BLOCK, axis=1)
        qib = lax.dynamic_slice_in_dim(q_idx, start, Q_BLOCK, axis=1)
        wb = lax.dynamic_slice_in_dim(w_scaled, start, Q_BLOCK, axis=1)
        qpos = start + jnp.arange(Q_BLOCK)
        causal = key_pos[None, :] <= qpos[:, None]
        logits = jnp.einsum('bthd,bsd->bths', qib, k_idx) * (IDX_DIM ** -0.5)
        score = jnp.einsum('bth,bths->bts', wb, jax.nn.relu(logits)).astype(jnp.float32)
        score = jnp.where(causal[None], score, -jnp.inf)
        _, idx = lax.top_k(score, top)
        valid = idx <= qpos[None, :, None]
        k_sel = gather(k, idx)
        v_sel = gather(v, idx)
        sc = jnp.einsum('bthd,btkd->bthk', qb, k_sel).astype(jnp.float32) * attn_scale
        sc = jnp.where(valid[:, :, None, :], sc, -jnp.inf)
        p = jax.nn.softmax(sc, axis=-1).astype(v.dtype)
        return jnp.einsum('bthk,btkd->bthd', p, v_sel)

    out = lax.map(block, jnp.arange(n_blk))
    return out.transpose(1, 0, 2, 3, 4).reshape(bn, s, B_WIDTH)


def hybrid_layer(x, cond, cos, sin, w_ada, b_ada, w_in, v_norm_g, v_norm_b,
                 w_spatial, b_spatial, w_out, ln_g, ln_b):
    bn, s, _ = x.shape
    mod = cond @ w_ada + b_ada
    shift, scale, gate = jnp.split(mod, 3, axis=-1)
    h = layer_norm(x) * (1.0 + scale[:, None, :]) + shift[:, None, :]
    proj = h @ w_in
    u, v, z_a, q, k, val, z_b, q_idx, k_idx, w_idx = jnp.split(proj, SPLIT_POINTS, axis=-1)
    y_a = jax.nn.silu(z_a) * spatial_gating_unit(u, v, v_norm_g, v_norm_b, w_spatial, b_spatial)
    cq, sq = cos[:, :, None, :], sin[:, :, None, :]
    q = apply_rope(q.reshape(bn, s, B_HEADS, HEAD_DIM), cq, sq)
    k = apply_rope(k, cos, sin)
    q_idx = apply_rope(q_idx.reshape(bn, s, IDX_HEADS, IDX_DIM), cq, sq)
    k_idx = apply_rope(k_idx, cos, sin)
    y_b = jax.nn.silu(z_b) * sparse_attention(q, k, val, q_idx, k_idx, w_idx)
    y = jnp.concatenate([y_a, y_b], axis=-1) @ w_out
    return layer_norm(ALPHA * x + gate[:, None, :] * y, ln_g, ln_b)


def setup_inputs(seed: int = 0) -> dict:
    key = jax.random.key(seed)
    ks = jax.random.split(key, 14)
    f32 = jnp.float32
    x = jax.random.normal(ks[0], (BATCH, SEQ, D_MODEL), f32)
    c = jax.random.normal(ks[1], (BATCH, D_MODEL), f32)
    offs = jax.random.randint(ks[2], (BATCH, 1), 0, 1024, dtype=jnp.int32)
    positions = (offs + jnp.arange(SEQ, dtype=jnp.int32)[None, :]).astype(jnp.int32)
    w_ada = 0.5 * D_MODEL ** -0.5 * jax.random.normal(ks[3], (DEPTH, D_MODEL, 3 * D_MODEL), f32)
    gate_one = jnp.concatenate([jnp.zeros((2 * D_MODEL,), f32), jnp.ones((D_MODEL,), f32)])
    b_ada = gate_one[None] + 0.02 * jax.random.normal(ks[4], (DEPTH, 3 * D_MODEL), f32)
    w_in = D_MODEL ** -0.5 * jax.random.normal(ks[5], (DEPTH, D_MODEL, D_IN), f32)
    v_norm_g = 1.0 + 0.02 * jax.random.normal(ks[6], (DEPTH, A_WIDTH), f32)
    v_norm_b = 0.02 * jax.random.normal(ks[7], (DEPTH, A_WIDTH), f32)
    w_spatial = CHUNK ** -0.5 * jax.random.normal(ks[8], (DEPTH, A_GROUPS, CHUNK, CHUNK), f32)
    b_spatial = 1.0 + 0.02 * jax.random.normal(ks[9], (DEPTH, A_GROUPS, CHUNK), f32)
    xavier = math.sqrt(2.0 / (D_MIX + D_MODEL))
    w_out = BETA * xavier * jax.random.normal(ks[10], (DEPTH, D_MIX, D_MODEL), f32)
    ln_g = 1.0 + 0.02 * jax.random.normal(ks[11], (DEPTH, D_MODEL), f32)
    ln_b = 0.02 * jax.random.normal(ks[12], (DEPTH, D_MODEL), f32)
    return {"x": x, "c": c, "positions": positions, "w_ada": w_ada, "b_ada": b_ada,
            "w_in": w_in, "v_norm_g": v_norm_g, "v_norm_b": v_norm_b,
            "w_spatial": w_spatial, "b_spatial": b_spatial, "w_out": w_out,
            "ln_g": ln_g, "ln_b": ln_b}


def reference(x, c, positions, w_ada, b_ada, w_in, v_norm_g, v_norm_b,
              w_spatial, b_spatial, w_out, ln_g, ln_b):
    cos, sin = rope_tables(positions)
    cond = jax.nn.silu(c)
    for l in range(DEPTH):
        x = hybrid_layer(x, cond, cos, sin, w_ada[l], b_ada[l], w_in[l], v_norm_g[l], v_norm_b[l],
                         w_spatial[l], b_spatial[l], w_out[l], ln_g[l], ln_b[l])
    return x
```

```python
import functools
import math

import numpy as np
import jax
import jax.numpy as jnp
from jax import lax
from jax.experimental import pallas as pl
from jax.experimental.pallas import tpu as pltpu

D_MODEL = 1024
A_WIDTH = D_MODEL // 2
A_GROUPS = 8
A_GDIM = A_WIDTH // A_GROUPS
CHUNK = 128
B_HEADS = 8
HEAD_DIM = 64
HALF = HEAD_DIM // 2
B_WIDTH = B_HEADS * HEAD_DIM
IDX_HEADS = 8
IDX_DIM = HEAD_DIM
TOPK_MAX = 256
Q_BLOCK = 128
ROPE_THETA = 10000.0
LN_EPS = 1e-5

LANES = 128
VMEM_LIMIT_BYTES = 56 * 1024 * 1024

SPLITS = (A_WIDTH, A_WIDTH, A_WIDTH, B_WIDTH, HEAD_DIM, HEAD_DIM, B_WIDTH,
          IDX_HEADS * IDX_DIM, IDX_DIM, IDX_HEADS)
D_IN = sum(SPLITS)
_OFF = np.concatenate([[0], np.cumsum(SPLITS)]).astype(np.int64)
U_OFF, V_OFF, ZA_OFF, Q_OFF, K_OFF, VAL_OFF, ZB_OFF, QI_OFF, KI_OFF, W_OFF = (int(o) for o in _OFF[:-1])

C_U, C_V, C_ZA, C_ZB, C_Q, C_QI = 0, 512, 1024, 1536, 2048, 2560
C_KK, C_KI, C_VW = 3072, 3200, 3328
D_PROJ = 3456
W_LANE = HEAD_DIM

KEY_CHUNK = 512
NEG_INF = float("-inf")
INT_MIN = -(2 ** 31)
KEY_OF_NEG_INF = INT_MIN + 0x007FFFFF
M_INIT = -1e30


def _pair_cols(off):
    cols = []
    for p in range(B_HEADS // 2):
        e, o = off + HEAD_DIM * (2 * p), off + HEAD_DIM * (2 * p + 1)
        cols += [np.arange(e, e + HALF), np.arange(o, o + HALF),
                 np.arange(e + HALF, e + HEAD_DIM), np.arange(o + HALF, o + HEAD_DIM)]
    return np.concatenate(cols)


def _dup_cols(off):
    a, b = np.arange(off, off + HALF), np.arange(off + HALF, off + HEAD_DIM)
    return np.concatenate([a, a, b, b])


def _proj_perm():
    zero = D_IN
    vw = np.concatenate([np.arange(VAL_OFF, VAL_OFF + HEAD_DIM), np.arange(W_OFF, W_OFF + IDX_HEADS),
                         np.full((LANES - HEAD_DIM - IDX_HEADS,), zero)])
    perm = np.concatenate([
        np.arange(U_OFF, U_OFF + A_WIDTH), np.arange(V_OFF, V_OFF + A_WIDTH),
        np.arange(ZA_OFF, ZA_OFF + A_WIDTH), np.arange(ZB_OFF, ZB_OFF + B_WIDTH),
        _pair_cols(Q_OFF), _pair_cols(QI_OFF), _dup_cols(K_OFF), _dup_cols(KI_OFF), vw])
    assert perm.shape[0] == D_PROJ
    return perm


def _silu(x):
    return x * (1.0 / (1.0 + jnp.exp(-x)))


def _dot_nt(a, b):
    return lax.dot_general(a, b, (((1,), (1,)), ((), ())), preferred_element_type=jnp.float32)


def _rope_kernel(pos_ref, invf_ref, sgn_ref, cos_ref, sin_ref):
    ang = pos_ref[...].astype(jnp.float32) * invf_ref[...]
    cos_ref[...] = jnp.cos(ang)
    sin_ref[...] = jnp.sin(ang) * sgn_ref[...]


def _rope_tables(positions):
    b, s = positions.shape
    ts = min(s, 1024)
    inv_freq = ROPE_THETA ** (-jnp.arange(0, HEAD_DIM, 2, dtype=jnp.float32) / HEAD_DIM)
    invf = jnp.tile(inv_freq, LANES // HALF)[None, :]
    sgn = jnp.concatenate([-jnp.ones((LANES // 2,), jnp.float32), jnp.ones((LANES // 2,), jnp.float32)])[None, :]
    out = jax.ShapeDtypeStruct((b, s, LANES), jnp.float32)
    return pl.pallas_call(
        _rope_kernel,
        out_shape=(out, out),
        grid=(b, s // ts),
        in_specs=[pl.BlockSpec((None, ts, 1), lambda i, j: (i, j, 0)),
                  pl.BlockSpec((1, LANES), lambda i, j: (0, 0)),
                  pl.BlockSpec((1, LANES), lambda i, j: (0, 0))],
        out_specs=(pl.BlockSpec((None, ts, LANES), lambda i, j: (i, j, 0)),
                   pl.BlockSpec((None, ts, LANES), lambda i, j: (i, j, 0))),
        name="rope_tables",
    )(positions.reshape(b, s, 1), invf, sgn)


def _mod_kernel(c_ref, w_ref, b_ref, o_ref):
    cond = _silu(c_ref[...])
    o_ref[...] = jnp.dot(cond, w_ref[...], preferred_element_type=jnp.float32,
                         precision=lax.Precision.HIGHEST) + b_ref[...]


def _modulation(c, w_ada, b_ada):
    depth, d, d3 = w_ada.shape
    b = c.shape[0]
    return pl.pallas_call(
        _mod_kernel,
        out_shape=jax.ShapeDtypeStruct((depth, b, d3), jnp.float32),
        grid=(depth, d3 // d),
        in_specs=[pl.BlockSpec((b, d), lambda l, j: (0, 0)),
                  pl.BlockSpec((None, d, d), lambda l, j: (l, 0, j)),
                  pl.BlockSpec((None, 1, d), lambda l, j: (l, 0, j))],
        out_specs=pl.BlockSpec((None, b, d), lambda l, j: (l, 0, j)),
        compiler_params=pltpu.CompilerParams(vmem_limit_bytes=VMEM_LIMIT_BYTES),
        name="adaln_mod",
    )(c, w_ada, b_ada.reshape(depth, 1, d3))


def _rope128(x, cos, sin_signed):
    return x * cos + pltpu.roll(x, LANES // 2, 1) * sin_signed


def _in_proj_kernel(x_ref, mod_ref, cos_ref, sin_ref, w_ref, gv_ref, bv_ref, ws_ref, bs_ref,
                    ya_ref, gb_ref, qr_ref, qi_ref, kk_ref, ki_ref, vx_ref, wi_ref):
    tm = x_ref.shape[0]
    x = x_ref[...]
    mu = jnp.mean(x, axis=-1, keepdims=True)
    xc = x - mu
    var = jnp.mean(xc * xc, axis=-1, keepdims=True)
    xn = xc * lax.rsqrt(var + LN_EPS)
    h = xn * (1.0 + mod_ref[1]) + mod_ref[0]
    hb = h.astype(jnp.bfloat16)

    def proj(c0, n):
        return jnp.dot(hb, w_ref[:, c0:c0 + n], preferred_element_type=jnp.float32)

    v = proj(C_V, A_WIDTH)
    vmu = jnp.mean(v, axis=-1, keepdims=True)
    vc = v - vmu
    vvar = jnp.mean(vc * vc, axis=-1, keepdims=True)
    vn = (vc * lax.rsqrt(vvar + LN_EPS) * gv_ref[...] + bv_ref[...]).astype(jnp.bfloat16)
    r_i = lax.broadcasted_iota(jnp.int32, (CHUNK, CHUNK), 0)
    c_i = lax.broadcasted_iota(jnp.int32, (CHUNK, CHUNK), 1)
    wm = [jnp.where(r_i >= c_i, ws_ref[g], 0.0).astype(jnp.bfloat16) for g in range(A_GROUPS)]
    gate_a = _silu(proj(C_ZA, A_WIDTH)) * proj(C_U, A_WIDTH)
    for c in range(tm // CHUNK):
        rows = slice(c * CHUNK, (c + 1) * CHUNK)
        mixed = jnp.concatenate(
            [jnp.dot(wm[g], vn[rows, g * A_GDIM:(g + 1) * A_GDIM], preferred_element_type=jnp.float32)
             for g in range(A_GROUPS)], axis=1) + bs_ref[...]
        ya_ref[rows, :] = (gate_a[rows, :] * mixed).astype(jnp.bfloat16)

    gb_ref[...] = _silu(proj(C_ZB, B_WIDTH)).astype(jnp.bfloat16)
    cos = cos_ref[...]
    sin = sin_ref[...]
    q_scale = HEAD_DIM ** -0.5 * math.log2(math.e)
    qi_scale = IDX_DIM ** -0.5
    q = proj(C_Q, B_WIDTH)
    qi = proj(C_QI, IDX_HEADS * IDX_DIM)
    for p in range(B_WIDTH // LANES):
        ls = slice(p * LANES, (p + 1) * LANES)
        qr_ref[:, ls] = (_rope128(q[:, ls], cos, sin) * q_scale).astype(jnp.bfloat16)
        qi_ref[:, ls] = (_rope128(qi[:, ls], cos, sin) * qi_scale).astype(jnp.bfloat16)
    kk_ref[...] = _rope128(proj(C_KK, LANES), cos, sin).astype(jnp.bfloat16)
    ki_ref[...] = _rope128(proj(C_KI, LANES), cos, sin).astype(jnp.bfloat16)
    vw = proj(C_VW, LANES)
    lane = lax.broadcasted_iota(jnp.int32, vw.shape, 1)
    vx_ref[...] = jnp.where(lane < HEAD_DIM, vw, jnp.where(lane == HEAD_DIM, 1.0, 0.0)).astype(jnp.bfloat16)
    wi_ref[...] = vw * (IDX_HEADS ** -0.5)


def _in_proj(x, mod4, cos, sin, w_p, gv, bv, ws, bs_full, tm):
    b, s, d = x.shape
    row = lambda i, j: (i, j, 0)
    const2 = lambda i, j: (0, 0)
    bf = jnp.bfloat16
    shapes = (
        jax.ShapeDtypeStruct((b, s, A_WIDTH), bf),
        jax.ShapeDtypeStruct((b, s, B_WIDTH), bf),
        jax.ShapeDtypeStruct((b, s, B_WIDTH), bf),
        jax.ShapeDtypeStruct((b, s, B_WIDTH), bf),
        jax.ShapeDtypeStruct((b, s, LANES), bf),
        jax.ShapeDtypeStruct((b, s, LANES), bf),
        jax.ShapeDtypeStruct((b, s, LANES), bf),
        jax.ShapeDtypeStruct((b, s, LANES), jnp.float32),
    )
    wide = pl.BlockSpec((None, tm, B_WIDTH), row)
    narrow = pl.BlockSpec((None, tm, LANES), row)
    return pl.pallas_call(
        _in_proj_kernel,
        out_shape=shapes,
        grid=(b, s // tm),
        in_specs=[pl.BlockSpec((None, tm, d), row),
                  pl.BlockSpec((None, 3, 1, d), lambda i, j: (i, 0, 0, 0)),
                  narrow, narrow,
                  pl.BlockSpec((d, D_PROJ), const2),
                  pl.BlockSpec((1, A_WIDTH), const2),
                  pl.BlockSpec((1, A_WIDTH), const2),
                  pl.BlockSpec((A_GROUPS, CHUNK, CHUNK), lambda i, j: (0, 0, 0)),
                  pl.BlockSpec((CHUNK, A_WIDTH), const2)],
        out_specs=(wide, wide, wide, wide, narrow, narrow, narrow, narrow),
        compiler_params=pltpu.CompilerParams(vmem_limit_bytes=VMEM_LIMIT_BYTES),
        name="in_proj",
    )(x, mod4, cos, sin, w_p, gv, bv, ws, bs_full)


def _key_to_float(key):
    bits = jnp.where(key >= 0, key, key ^ 0x7FFFFFFF)
    return lax.bitcast_convert_type(bits, jnp.float32)


def _attn_kernel(qi_ref, qr_ref, wi_ref, gb_ref, ki_ref, kk_ref, vx_ref, yb_ref,
                 score_ref, qim_ref, qrm_ref, m_ref, acc_ref, *, top_k):
    i = pl.program_id(1)
    tq = Q_BLOCK
    ck = KEY_CHUNK
    n_chunks = (i * tq) // ck + 1
    lanes_per_chunk = ck // LANES
    f32 = jnp.float32

    row = i * tq + lax.broadcasted_iota(jnp.int32, (tq, 1), 0)
    lane = lax.broadcasted_iota(jnp.int32, (tq, LANES), 1)
    even = (lane % HEAD_DIM) < HALF

    for p in range(B_HEADS // 2):
        ls = slice(p * LANES, (p + 1) * LANES)
        for src, dst in ((qi_ref, qim_ref), (qr_ref, qrm_ref)):
            pair = src[:, ls]
            zero = jnp.zeros_like(pair)
            dst[2 * p] = jnp.where(even, pair, zero)
            dst[2 * p + 1] = jnp.where(even, zero, pair)

    def chunk_off(c):
        return pl.multiple_of(c * ck, ck)

    def idx_body(c, carry):
        off = chunk_off(c)
        kc = ki_ref[pl.ds(off, ck), :]
        acc = None
        for h in range(IDX_HEADS):
            logit = _dot_nt(qim_ref[h], kc)
            term = wi_ref[:, W_LANE + h:W_LANE + h + 1] * jnp.maximum(logit, 0.0)
            acc = term if acc is None else acc + term
        col = off + lax.broadcasted_iota(jnp.int32, (1, ck), 1)
        score_ref[:, pl.ds(off, ck)] = jnp.where(col <= row, acc, NEG_INF)
        return carry

    lax.fori_loop(0, n_chunks, idx_body, 0)

    def count_ge(thr):
        thr_b = jnp.broadcast_to(thr, (tq, LANES))

        def body(c, acc):
            s = score_ref[:, pl.ds(chunk_off(c), ck)]
            for j in range(lanes_per_chunk):
                acc = acc + jnp.where(s[:, j * LANES:(j + 1) * LANES] >= thr_b, 1.0, 0.0)
            return acc

        acc = lax.fori_loop(0, n_chunks, body, jnp.zeros((tq, LANES), f32))
        return jnp.sum(acc, axis=1, keepdims=True)

    kf = float(top_k)
    nonneg = count_ge(jnp.zeros((tq, 1), f32)) >= kf
    key0 = jnp.where(nonneg, 0, INT_MIN).astype(jnp.int32)

    def bit_body(b, key):
        cand = key | jnp.left_shift(jnp.int32(1), 30 - b)
        return jnp.where(count_ge(_key_to_float(cand)) >= kf, cand, key)

    key = lax.fori_loop(0, 31, bit_body, key0)
    key = jnp.maximum(key, KEY_OF_NEG_INF)
    thr = _key_to_float(key)
    thr_next = _key_to_float(key + 1)
    n_ge = count_ge(thr)
    need = kf - count_ge(thr_next)
    has_tie = n_ge > kf

    def tie_search():
        thr_b = jnp.broadcast_to(thr, (tq, LANES))
        nxt_b = jnp.broadcast_to(thr_next, (tq, LANES))

        def count_ties_below(bound):
            bound_b = jnp.broadcast_to(bound, (tq, LANES))

            def body(c, acc):
                off = chunk_off(c)
                s = score_ref[:, pl.ds(off, ck)]
                for j in range(lanes_per_chunk):
                    sj = s[:, j * LANES:(j + 1) * LANES]
                    tie = jnp.where(sj >= thr_b, jnp.where(sj >= nxt_b, 0.0, 1.0), 0.0)
                    acc = acc + jnp.where(off + j * LANES + lane < bound_b, tie, 0.0)
                return acc

            acc = lax.fori_loop(0, n_chunks, body, jnp.zeros((tq, LANES), f32))
            return jnp.sum(acc, axis=1, keepdims=True)

        n_bits = max(1, int(score_ref.shape[1] - 1).bit_length())

        def bit_body2(b, bound):
            cand = bound | jnp.left_shift(jnp.int32(1), n_bits - 1 - b)
            return jnp.where(count_ties_below(cand) < need, cand, bound)

        return lax.fori_loop(0, n_bits, bit_body2, jnp.zeros((tq, 1), jnp.int32))

    any_tie = jnp.max(jnp.where(has_tie, 1.0, 0.0)) > 0.0
    last_tie = lax.cond(any_tie, tie_search, lambda: jnp.zeros((tq, 1), jnp.int32))
    last_tie = jnp.where(has_tie, last_tie, jnp.int32(score_ref.shape[1]))

    m_ref[...] = jnp.full(m_ref.shape, M_INIT, f32)
    acc_ref[...] = jnp.zeros(acc_ref.shape, f32)

    def att_body(c, carry):
        off = chunk_off(c)
        s_idx = score_ref[:, pl.ds(off, ck)]
        col = off + lax.broadcasted_iota(jnp.int32, (1, ck), 1)
        keep_tie = jnp.where(col <= last_tie, 0.0, NEG_INF)
        bias = jnp.where(s_idx >= thr, jnp.where(s_idx >= thr_next, 0.0, keep_tie), NEG_INF)
        bias = jnp.where(col <= row, bias, NEG_INF)
        kc = kk_ref[pl.ds(off, ck), :]
        vc = vx_ref[pl.ds(off, ck), :]
        for h in range(B_HEADS):
            s = _dot_nt(qrm_ref[h], kc) + bias
            m_old = m_ref[h]
            m_new = jnp.maximum(m_old, jnp.max(s, axis=1, keepdims=True))
            alpha = jnp.exp2(m_old - m_new)
            p = jnp.concatenate(
                [jnp.exp2(s[:, j * LANES:(j + 1) * LANES] - m_new) for j in range(lanes_per_chunk)],
                axis=1).astype(jnp.bfloat16)
            acc_ref[h] = acc_ref[h] * alpha + jnp.dot(p, vc, preferred_element_type=f32)
            m_ref[h] = m_new
        return carry

    lax.fori_loop(0, n_chunks, att_body, 0)

    for p in range(B_HEADS // 2):
        outs = []
        for h in (2 * p, 2 * p + 1):
            a = acc_ref[h]
            denom = jnp.sum(jnp.where(lane == HEAD_DIM, a, 0.0), axis=1, keepdims=True)
            outs.append(a / denom)
        pair = jnp.where(lane < HEAD_DIM, outs[0], pltpu.roll(outs[1], HEAD_DIM, 1))
        ls = slice(p * LANES, (p + 1) * LANES)
        yb_ref[:, ls] = (pair * gb_ref[:, ls].astype(f32)).astype(jnp.bfloat16)


def _sparse_attn(qi, qr, wi, gb, ki, kk, vx):
    b, s, _ = qr.shape
    top_k = min(TOPK_MAX, s // 4)
    blk = lambda i, j: (i, j, 0)
    full = lambda i, j: (i, 0, 0)
    wide = pl.BlockSpec((None, Q_BLOCK, B_WIDTH), blk)
    keys = pl.BlockSpec((None, s, LANES), full)
    return pl.pallas_call(
        functools.partial(_attn_kernel, top_k=top_k),
        out_shape=jax.ShapeDtypeStruct((b, s, B_WIDTH), jnp.bfloat16),
        grid=(b, s // Q_BLOCK),
        in_specs=[wide, wide, pl.BlockSpec((None, Q_BLOCK, LANES), blk), wide, keys, keys, keys],
        out_specs=wide,
        scratch_shapes=[pltpu.VMEM((Q_BLOCK, s), jnp.float32),
                        pltpu.VMEM((IDX_HEADS, Q_BLOCK, LANES), jnp.bfloat16),
                        pltpu.VMEM((B_HEADS, Q_BLOCK, LANES), jnp.bfloat16),
                        pltpu.VMEM((B_HEADS, Q_BLOCK, LANES), jnp.float32),
                        pltpu.VMEM((B_HEADS, Q_BLOCK, LANES), jnp.float32)],
        compiler_params=pltpu.CompilerParams(vmem_limit_bytes=VMEM_LIMIT_BYTES),
        name="sparse_attn",
    )(qi, qr, wi, gb, ki, kk, vx)


def _out_proj_kernel(x_ref, ya_ref, yb_ref, w_ref, mod_ref, g_ref, b_ref, o_ref, *, alpha):
    y = (jnp.dot(ya_ref[...], w_ref[:A_WIDTH, :], preferred_element_type=jnp.float32)
         + jnp.dot(yb_ref[...], w_ref[A_WIDTH:, :], preferred_element_type=jnp.float32))
    r = alpha * x_ref[...] + mod_ref[2] * y
    mu = jnp.mean(r, axis=-1, keepdims=True)
    rc = r - mu
    var = jnp.mean(rc * rc, axis=-1, keepdims=True)
    o_ref[...] = rc * lax.rsqrt(var + LN_EPS) * g_ref[...] + b_ref[...]


def _out_proj(x, ya, yb, w_out, mod4, ln_g, ln_b, alpha, tm):
    b, s, d = x.shape
    row = lambda i, j: (i, j, 0)
    const2 = lambda i, j: (0, 0)
    return pl.pallas_call(
        functools.partial(_out_proj_kernel, alpha=alpha),
        out_shape=jax.ShapeDtypeStruct((b, s, d), jnp.float32),
        grid=(b, s // tm),
        in_specs=[pl.BlockSpec((None, tm, d), row),
                  pl.BlockSpec((None, tm, A_WIDTH), row),
                  pl.BlockSpec((None, tm, B_WIDTH), row),
                  pl.BlockSpec((A_WIDTH + B_WIDTH, d), const2),
                  pl.BlockSpec((None, 3, 1, d), lambda i, j: (i, 0, 0, 0)),
                  pl.BlockSpec((1, d), const2),
                  pl.BlockSpec((1, d), const2)],
        out_specs=pl.BlockSpec((None, tm, d), row),
        compiler_params=pltpu.CompilerParams(vmem_limit_bytes=VMEM_LIMIT_BYTES),
        name="out_proj",
    )(x, ya, yb, w_out, mod4, ln_g, ln_b)


def kernel(x, c, positions, w_ada, b_ada, w_in, v_norm_g, v_norm_b, w_spatial, b_spatial, w_out, ln_g, ln_b):
    depth, d, _ = w_in.shape
    b, s, _ = x.shape
    assert d == D_MODEL and s % KEY_CHUNK == 0
    tm = min(512, s)
    alpha = (2.0 * depth) ** 0.25

    cos, sin = _rope_tables(positions)
    mod = _modulation(c, w_ada, b_ada)
    mod4 = mod.reshape(depth, b, 3, 1, d)

    w_ext = jnp.concatenate([w_in, jnp.zeros((depth, d, 1), w_in.dtype)], axis=2)
    w_p = jnp.take(w_ext, jnp.asarray(_proj_perm(), jnp.int32), axis=2).astype(jnp.bfloat16)
    w_o = w_out.astype(jnp.bfloat16)
    bs_full = jnp.repeat(jnp.swapaxes(b_spatial, 1, 2), A_GDIM, axis=2)

    for l in range(depth):
        ya, gb, qr, qi, kk, ki, vx, wi = _in_proj(
            x, mod4[l], cos, sin, w_p[l], v_norm_g[l][None, :], v_norm_b[l][None, :],
            w_spatial[l], bs_full[l], tm)
        yb = _sparse_attn(qi, qr, wi, gb, ki, kk, vx)
        x = _out_proj(x, ya, yb, w_o[l], mod4[l], ln_g[l][None, :], ln_b[l][None, :], alpha, tm)
    return x
```

```python
import functools
import math

import numpy as np
import jax
import jax.numpy as jnp
from jax import lax
from jax.experimental import pallas as pl
from jax.experimental.pallas import tpu as pltpu

D_MODEL = 1024
A_WIDTH = D_MODEL // 2
A_GROUPS = 8
A_GDIM = A_WIDTH // A_GROUPS
CHUNK = 128
B_HEADS = 8
HEAD_DIM = 64
HALF = HEAD_DIM // 2
B_WIDTH = B_HEADS * HEAD_DIM
IDX_HEADS = 8
IDX_DIM = HEAD_DIM
TOPK_MAX = 256
Q_BLOCK = 128
ROPE_THETA = 10000.0
LN_EPS = 1e-5

LANES = 128
VMEM_LIMIT_BYTES = 56 * 1024 * 1024

SPLITS = (A_WIDTH, A_WIDTH, A_WIDTH, B_WIDTH, HEAD_DIM, HEAD_DIM, B_WIDTH,
          IDX_HEADS * IDX_DIM, IDX_DIM, IDX_HEADS)
D_IN = sum(SPLITS)
_OFF = np.concatenate([[0], np.cumsum(SPLITS)]).astype(np.int64)
U_OFF, V_OFF, ZA_OFF, Q_OFF, K_OFF, VAL_OFF, ZB_OFF, QI_OFF, KI_OFF, W_OFF = (int(o) for o in _OFF[:-1])

C_U, C_V, C_ZA, C_ZB, C_Q, C_QI = 0, 512, 1024, 1536, 2048, 2560
C_KK, C_KI, C_VW = 3072, 3200, 3328
D_PROJ = 3456
W_LANE = HEAD_DIM

KEY_CHUNK = 512
COUNT_ROWS = 64
NEG_INF = float("-inf")
INT_MIN = -(2 ** 31)
KEY_OF_NEG_INF = INT_MIN + 0x007FFFFF
M_INIT = -1e30


def _pair_cols(off):
    cols = []
    for p in range(B_HEADS // 2):
        e, o = off + HEAD_DIM * (2 * p), off + HEAD_DIM * (2 * p + 1)
        cols += [np.arange(e, e + HALF), np.arange(o, o + HALF),
                 np.arange(e + HALF, e + HEAD_DIM), np.arange(o + HALF, o + HEAD_DIM)]
    return np.concatenate(cols)


def _dup_cols(off):
    a, b = np.arange(off, off + HALF), np.arange(off + HALF, off + HEAD_DIM)
    return np.concatenate([a, a, b, b])


def _proj_perm():
    zero = D_IN
    vw = np.concatenate([np.arange(VAL_OFF, VAL_OFF + HEAD_DIM), np.arange(W_OFF, W_OFF + IDX_HEADS),
                         np.full((LANES - HEAD_DIM - IDX_HEADS,), zero)])
    perm = np.concatenate([
        np.arange(U_OFF, U_OFF + A_WIDTH), np.arange(V_OFF, V_OFF + A_WIDTH),
        np.arange(ZA_OFF, ZA_OFF + A_WIDTH), np.arange(ZB_OFF, ZB_OFF + B_WIDTH),
        _pair_cols(Q_OFF), _pair_cols(QI_OFF), _dup_cols(K_OFF), _dup_cols(KI_OFF), vw])
    assert perm.shape[0] == D_PROJ
    return perm


def _silu(x):
    return x * (1.0 / (1.0 + jnp.exp(-x)))


def _dot_nt(a, b):
    return lax.dot_general(a, b, (((1,), (1,)), ((), ())), preferred_element_type=jnp.float32)


def _rope_kernel(pos_ref, invf_ref, sgn_ref, cos_ref, sin_ref):
    ang = pos_ref[...].astype(jnp.float32) * invf_ref[...]
    cos_ref[...] = jnp.cos(ang)
    sin_ref[...] = jnp.sin(ang) * sgn_ref[...]


def _rope_tables(positions):
    b, s = positions.shape
    ts = min(s, 1024)
    inv_freq = ROPE_THETA ** (-jnp.arange(0, HEAD_DIM, 2, dtype=jnp.float32) / HEAD_DIM)
    invf = jnp.tile(inv_freq, LANES // HALF)[None, :]
    sgn = jnp.concatenate([-jnp.ones((LANES // 2,), jnp.float32), jnp.ones((LANES // 2,), jnp.float32)])[None, :]
    out = jax.ShapeDtypeStruct((b, s, LANES), jnp.float32)
    return pl.pallas_call(
        _rope_kernel,
        out_shape=(out, out),
        grid=(b, s // ts),
        in_specs=[pl.BlockSpec((None, ts, 1), lambda i, j: (i, j, 0)),
                  pl.BlockSpec((1, LANES), lambda i, j: (0, 0)),
                  pl.BlockSpec((1, LANES), lambda i, j: (0, 0))],
        out_specs=(pl.BlockSpec((None, ts, LANES), lambda i, j: (i, j, 0)),
                   pl.BlockSpec((None, ts, LANES), lambda i, j: (i, j, 0))),
        name="rope_tables",
    )(positions.reshape(b, s, 1), invf, sgn)


def _mod_kernel(c_ref, w_ref, b_ref, o_ref):
    cond = _silu(c_ref[...])
    o_ref[...] = jnp.dot(cond, w_ref[...], preferred_element_type=jnp.float32,
                         precision=lax.Precision.HIGHEST) + b_ref[...]


def _modulation(c, w_ada, b_ada):
    depth, d, d3 = w_ada.shape
    b = c.shape[0]
    return pl.pallas_call(
        _mod_kernel,
        out_shape=jax.ShapeDtypeStruct((depth, b, d3), jnp.float32),
        grid=(depth, d3 // d),
        in_specs=[pl.BlockSpec((b, d), lambda l, j: (0, 0)),
                  pl.BlockSpec((None, d, d), lambda l, j: (l, 0, j)),
                  pl.BlockSpec((None, 1, d), lambda l, j: (l, 0, j))],
        out_specs=pl.BlockSpec((None, b, d), lambda l, j: (l, 0, j)),
        compiler_params=pltpu.CompilerParams(vmem_limit_bytes=VMEM_LIMIT_BYTES),
        name="adaln_mod",
    )(c, w_ada, b_ada.reshape(depth, 1, d3))


def _rope128(x, cos, sin_signed):
    return x * cos + pltpu.roll(x, LANES // 2, 1) * sin_signed


def _in_proj_kernel(x_ref, mod_ref, cos_ref, sin_ref, w_ref, gv_ref, bv_ref, ws_ref, bs_ref,
                    ya_ref, gb_ref, qr_ref, qi_ref, kk_ref, ki_ref, vx_ref, wi_ref):
    tm = x_ref.shape[0]
    x = x_ref[...]
    mu = jnp.mean(x, axis=-1, keepdims=True)
    xc = x - mu
    var = jnp.mean(xc * xc, axis=-1, keepdims=True)
    xn = xc * lax.rsqrt(var + LN_EPS)
    h = xn * (1.0 + mod_ref[1]) + mod_ref[0]
    hb = h.astype(jnp.bfloat16)

    def proj(c0, n):
        return jnp.dot(hb, w_ref[:, c0:c0 + n], preferred_element_type=jnp.float32)

    v = proj(C_V, A_WIDTH)
    vmu = jnp.mean(v, axis=-1, keepdims=True)
    vc = v - vmu
    vvar = jnp.mean(vc * vc, axis=-1, keepdims=True)
    vn = (vc * lax.rsqrt(vvar + LN_EPS) * gv_ref[...] + bv_ref[...]).astype(jnp.bfloat16)
    r_i = lax.broadcasted_iota(jnp.int32, (CHUNK, CHUNK), 0)
    c_i = lax.broadcasted_iota(jnp.int32, (CHUNK, CHUNK), 1)
    wm = [jnp.where(r_i >= c_i, ws_ref[g], 0.0).astype(jnp.bfloat16) for g in range(A_GROUPS)]
    gate_a = _silu(proj(C_ZA, A_WIDTH)) * proj(C_U, A_WIDTH)
    for c in range(tm // CHUNK):
        rows = slice(c * CHUNK, (c + 1) * CHUNK)
        mixed = jnp.concatenate(
            [jnp.dot(wm[g], vn[rows, g * A_GDIM:(g + 1) * A_GDIM], preferred_element_type=jnp.float32)
             for g in range(A_GROUPS)], axis=1) + bs_ref[...]
        ya_ref[rows, :] = (gate_a[rows, :] * mixed).astype(jnp.bfloat16)

    gb_ref[...] = _silu(proj(C_ZB, B_WIDTH)).astype(jnp.bfloat16)
    cos = cos_ref[...]
    sin = sin_ref[...]
    q_scale = HEAD_DIM ** -0.5 * math.log2(math.e)
    qi_scale = IDX_DIM ** -0.5
    q = proj(C_Q, B_WIDTH)
    qi = proj(C_QI, IDX_HEADS * IDX_DIM)
    for p in range(B_WIDTH // LANES):
        ls = slice(p * LANES, (p + 1) * LANES)
        qr_ref[:, ls] = (_rope128(q[:, ls], cos, sin) * q_scale).astype(jnp.bfloat16)
        qi_ref[:, ls] = (_rope128(qi[:, ls], cos, sin) * qi_scale).astype(jnp.bfloat16)
    kk_ref[...] = _rope128(proj(C_KK, LANES), cos, sin).astype(jnp.bfloat16)
    ki_ref[...] = _rope128(proj(C_KI, LANES), cos, sin).astype(jnp.bfloat16)
    vw = proj(C_VW, LANES)
    lane = lax.broadcasted_iota(jnp.int32, vw.shape, 1)
    vx_ref[...] = jnp.where(lane < HEAD_DIM, vw, jnp.where(lane == HEAD_DIM, 1.0, 0.0)).astype(jnp.bfloat16)
    wi_ref[...] = vw * (IDX_HEADS ** -0.5)


def _in_proj(x, mod4, cos, sin, w_p, gv, bv, ws, bs_full, tm):
    b, s, d = x.shape
    row = lambda i, j: (i, j, 0)
    const2 = lambda i, j: (0, 0)
    bf = jnp.bfloat16
    shapes = (
        jax.ShapeDtypeStruct((b, s, A_WIDTH), bf),
        jax.ShapeDtypeStruct((b, s, B_WIDTH), bf),
        jax.ShapeDtypeStruct((b, s, B_WIDTH), bf),
        jax.ShapeDtypeStruct((b, s, B_WIDTH), bf),
        jax.ShapeDtypeStruct((b, s, LANES), bf),
        jax.ShapeDtypeStruct((b, s, LANES), bf),
        jax.ShapeDtypeStruct((b, s, LANES), bf),
        jax.ShapeDtypeStruct((b, s, LANES), jnp.float32),
    )
    wide = pl.BlockSpec((None, tm, B_WIDTH), row)
    narrow = pl.BlockSpec((None, tm, LANES), row)
    return pl.pallas_call(
        _in_proj_kernel,
        out_shape=shapes,
        grid=(b, s // tm),
        in_specs=[pl.BlockSpec((None, tm, d), row),
                  pl.BlockSpec((None, 3, 1, d), lambda i, j: (i, 0, 0, 0)),
                  narrow, narrow,
                  pl.BlockSpec((d, D_PROJ), const2),
                  pl.BlockSpec((1, A_WIDTH), const2),
                  pl.BlockSpec((1, A_WIDTH), const2),
                  pl.BlockSpec((A_GROUPS, CHUNK, CHUNK), lambda i, j: (0, 0, 0)),
                  pl.BlockSpec((CHUNK, A_WIDTH), const2)],
        out_specs=(wide, wide, wide, wide, narrow, narrow, narrow, narrow),
        compiler_params=pltpu.CompilerParams(vmem_limit_bytes=VMEM_LIMIT_BYTES),
        name="in_proj",
    )(x, mod4, cos, sin, w_p, gv, bv, ws, bs_full)


def _key_to_float(key):
    bits = jnp.where(key >= 0, key, key ^ 0x7FFFFFFF)
    return lax.bitcast_convert_type(bits, jnp.float32)


def _attn_kernel(qi_ref, qr_ref, wi_ref, gb_ref, ki_ref, kk_ref, vx_ref, yb_ref,
                 score_ref, qim_ref, qrm_ref, wt_ref, m_ref, acc_ref, *, top_k):
    i = pl.program_id(1)
    tq = Q_BLOCK
    ck = KEY_CHUNK
    n_keys = score_ref.shape[0]
    n_chunks = (i * tq) // ck + 1
    lanes_per_chunk = ck // LANES
    f32 = jnp.float32

    lane = lax.broadcasted_iota(jnp.int32, (tq, LANES), 1)
    even = (lane % HEAD_DIM) < HALF
    q_pos = i * tq + lax.broadcasted_iota(jnp.int32, (1, tq), 1)
    key_iota = lax.broadcasted_iota(jnp.int32, (ck, 1), 0)

    for p in range(B_HEADS // 2):
        ls = slice(p * LANES, (p + 1) * LANES)
        for src, dst in ((qi_ref, qim_ref), (qr_ref, qrm_ref)):
            pair = src[:, ls]
            zero = jnp.zeros_like(pair)
            dst[2 * p * tq:(2 * p + 1) * tq, :] = jnp.where(even, pair, zero)
            dst[(2 * p + 1) * tq:(2 * p + 2) * tq, :] = jnp.where(even, zero, pair)
    wt_ref[...] = wi_ref[...].T

    def chunk_off(c):
        return pl.multiple_of(c * ck, ck)

    def idx_body(c, carry):
        off = chunk_off(c)
        logit_t = _dot_nt(ki_ref[pl.ds(off, ck), :], qim_ref[...])
        acc = None
        for h in range(IDX_HEADS):
            term = wt_ref[W_LANE + h:W_LANE + h + 1, :] * jnp.maximum(logit_t[:, h * tq:(h + 1) * tq], 0.0)
            acc = term if acc is None else acc + term
        score_ref[pl.ds(off, ck), :] = jnp.where(off + key_iota <= q_pos, acc, NEG_INF)
        return carry

    lax.fori_loop(0, n_chunks, idx_body, 0)

    def count_where(ind_fn):
        def body(c, acc):
            off = chunk_off(c)
            ind = ind_fn(score_ref[pl.ds(off, ck), :], off)
            return acc + jnp.sum(ind.reshape(ck // COUNT_ROWS, COUNT_ROWS, tq), axis=0)

        acc = lax.fori_loop(0, n_chunks, body, jnp.zeros((COUNT_ROWS, tq), f32))
        return jnp.sum(acc, axis=0, keepdims=True)

    def count_ge(thr):
        return count_where(lambda s, off: jnp.where(s >= thr, 1.0, 0.0))

    kf = float(top_k)
    nonneg = count_ge(jnp.zeros((1, tq), f32)) >= kf
    key0 = jnp.where(nonneg, 0, INT_MIN).astype(jnp.int32)

    def bit_body(b, key):
        cand = key | jnp.left_shift(jnp.int32(1), 30 - b)
        return jnp.where(count_ge(_key_to_float(cand)) >= kf, cand, key)

    key = lax.fori_loop(0, 31, bit_body, key0)
    key = jnp.maximum(key, KEY_OF_NEG_INF)
    thr = _key_to_float(key)
    thr_next = _key_to_float(key + 1)
    n_ge = count_ge(thr)
    need = kf - count_ge(thr_next)
    has_tie = n_ge > kf

    def tie_search():
        def count_ties_below(bound):
            def ind(s, off):
                tie = jnp.where(s >= thr, jnp.where(s >= thr_next, 0.0, 1.0), 0.0)
                return jnp.where(off + key_iota < bound, tie, 0.0)
            return count_where(ind)

        n_bits = max(1, int(n_keys - 1).bit_length())

        def bit_body2(b, bound):
            cand = bound | jnp.left_shift(jnp.int32(1), n_bits - 1 - b)
            return jnp.where(count_ties_below(cand) < need, cand, bound)

        return lax.fori_loop(0, n_bits, bit_body2, jnp.zeros((1, tq), jnp.int32))

    any_tie = jnp.max(jnp.where(has_tie, 1.0, 0.0)) > 0.0
    last_tie = lax.cond(any_tie, tie_search, lambda: jnp.zeros((1, tq), jnp.int32))
    last_tie = jnp.where(has_tie, last_tie, jnp.int32(n_keys))

    m_ref[...] = jnp.full(m_ref.shape, M_INIT, f32)
    acc_ref[...] = jnp.zeros(acc_ref.shape, f32)

    def att_body(c, carry):
        off = chunk_off(c)
        s_idx = score_ref[pl.ds(off, ck), :]
        key_pos = off + key_iota
        keep_tie = jnp.where(key_pos <= last_tie, 0.0, NEG_INF)
        bias_t = jnp.where(s_idx >= thr, jnp.where(s_idx >= thr_next, 0.0, keep_tie), NEG_INF)
        bias_t = jnp.where(key_pos <= q_pos, bias_t, NEG_INF)
        bias = bias_t.T
        s = _dot_nt(qrm_ref[...], kk_ref[pl.ds(off, ck), :])
        s = (s.reshape(B_HEADS, tq, ck) + bias[None]).reshape(B_HEADS * tq, ck)
        m_old = m_ref[...]
        m_new = jnp.maximum(m_old, jnp.max(s, axis=1, keepdims=True))
        p = jnp.concatenate(
            [jnp.exp2(s[:, j * LANES:(j + 1) * LANES] - m_new) for j in range(lanes_per_chunk)],
            axis=1).astype(jnp.bfloat16)
        acc_ref[...] = acc_ref[...] * jnp.exp2(m_old - m_new) + jnp.dot(
            p, vx_ref[pl.ds(off, ck), :], preferred_element_type=f32)
        m_ref[...] = m_new
        return carry

    lax.fori_loop(0, n_chunks, att_body, 0)

    for p in range(B_HEADS // 2):
        outs = []
        for h in (2 * p, 2 * p + 1):
            a = acc_ref[h * tq:(h + 1) * tq, :]
            denom = jnp.sum(jnp.where(lane == HEAD_DIM, a, 0.0), axis=1, keepdims=True)
            outs.append(a / denom)
        pair = jnp.where(lane < HEAD_DIM, outs[0], pltpu.roll(outs[1], HEAD_DIM, 1))
        ls = slice(p * LANES, (p + 1) * LANES)
        yb_ref[:, ls] = (pair * gb_ref[:, ls].astype(f32)).astype(jnp.bfloat16)


def _sparse_attn(qi, qr, wi, gb, ki, kk, vx):
    b, s, _ = qr.shape
    top_k = min(TOPK_MAX, s // 4)
    blk = lambda i, j: (i, j, 0)
    full = lambda i, j: (i, 0, 0)
    wide = pl.BlockSpec((None, Q_BLOCK, B_WIDTH), blk)
    keys = pl.BlockSpec((None, s, LANES), full)
    return pl.pallas_call(
        functools.partial(_attn_kernel, top_k=top_k),
        out_shape=jax.ShapeDtypeStruct((b, s, B_WIDTH), jnp.bfloat16),
        grid=(b, s // Q_BLOCK),
        in_specs=[wide, wide, pl.BlockSpec((None, Q_BLOCK, LANES), blk), wide, keys, keys, keys],
        out_specs=wide,
        scratch_shapes=[pltpu.VMEM((s, Q_BLOCK), jnp.float32),
                        pltpu.VMEM((IDX_HEADS * Q_BLOCK, LANES), jnp.bfloat16),
                        pltpu.VMEM((B_HEADS * Q_BLOCK, LANES), jnp.bfloat16),
                        pltpu.VMEM((LANES, Q_BLOCK), jnp.float32),
                        pltpu.VMEM((B_HEADS * Q_BLOCK, LANES), jnp.float32),
                        pltpu.VMEM((B_HEADS * Q_BLOCK, LANES), jnp.float32)],
        compiler_params=pltpu.CompilerParams(vmem_limit_bytes=VMEM_LIMIT_BYTES),
        name="sparse_attn",
    )(qi, qr, wi, gb, ki, kk, vx)


def _out_proj_kernel(x_ref, ya_ref, yb_ref, w_ref, mod_ref, g_ref, b_ref, o_ref, *, alpha):
    y = (jnp.dot(ya_ref[...], w_ref[:A_WIDTH, :], preferred_element_type=jnp.float32)
         + jnp.dot(yb_ref[...], w_ref[A_WIDTH:, :], preferred_element_type=jnp.float32))
    r = alpha * x_ref[...] + mod_ref[2] * y
    mu = jnp.mean(r, axis=-1, keepdims=True)
    rc = r - mu
    var = jnp.mean(rc * rc, axis=-1, keepdims=True)
    o_ref[...] = rc * lax.rsqrt(var + LN_EPS) * g_ref[...] + b_ref[...]


def _out_proj(x, ya, yb, w_out, mod4, ln_g, ln_b, alpha, tm):
    b, s, d = x.shape
    row = lambda i, j: (i, j, 0)
    const2 = lambda i, j: (0, 0)
    return pl.pallas_call(
        functools.partial(_out_proj_kernel, alpha=alpha),
        out_shape=jax.ShapeDtypeStruct((b, s, d), jnp.float32),
        grid=(b, s // tm),
        in_specs=[pl.BlockSpec((None, tm, d), row),
                  pl.BlockSpec((None, tm, A_WIDTH), row),
                  pl.BlockSpec((None, tm, B_WIDTH), row),
                  pl.BlockSpec((A_WIDTH + B_WIDTH, d), const2),
                  pl.BlockSpec((None, 3, 1, d), lambda i, j: (i, 0, 0, 0)),
                  pl.BlockSpec((1, d), const2),
                  pl.BlockSpec((1, d), const2)],
        out_specs=pl.BlockSpec((None, tm, d), row),
        compiler_params=pltpu.CompilerParams(vmem_limit_bytes=VMEM_LIMIT_BYTES),
        name="out_proj",
    )(x, ya, yb, w_out, mod4, ln_g, ln_b)


def kernel(x, c, positions, w_ada, b_ada, w_in, v_norm_g, v_norm_b, w_spatial, b_spatial, w_out, ln_g, ln_b):
    depth, d, _ = w_in.shape
    b, s, _ = x.shape
    assert d == D_MODEL and s % KEY_CHUNK == 0
    tm = min(512, s)
    alpha = (2.0 * depth) ** 0.25

    cos, sin = _rope_tables(positions)
    mod = _modulation(c, w_ada, b_ada)
    mod4 = mod.reshape(depth, b, 3, 1, d)

    w_ext = jnp.concatenate([w_in, jnp.zeros((depth, d, 1), w_in.dtype)], axis=2)
    w_p = jnp.take(w_ext, jnp.asarray(_proj_perm(), jnp.int32), axis=2).astype(jnp.bfloat16)
    w_o = w_out.astype(jnp.bfloat16)
    bs_full = jnp.repeat(jnp.swapaxes(b_spatial, 1, 2), A_GDIM, axis=2)

    for l in range(depth):
        ya, gb, qr, qi, kk, ki, vx, wi = _in_proj(
            x, mod4[l], cos, sin, w_p[l], v_norm_g[l][None, :], v_norm_b[l][None, :],
            w_spatial[l], bs_full[l], tm)
        yb = _sparse_attn(qi, qr, wi, gb, ki, kk, vx)
        x = _out_proj(x, ya, yb, w_o[l], mod4[l], ln_g[l][None, :], ln_b[l][None, :], alpha, tm)
    return x
```

```python
import functools
import math

import numpy as np
import jax
import jax.numpy as jnp
from jax import lax
from jax.experimental import pallas as pl
from jax.experimental.pallas import tpu as pltpu

D_MODEL = 1024
A_WIDTH = D_MODEL // 2
A_GROUPS = 8
A_GDIM = A_WIDTH // A_GROUPS
CHUNK = 128
B_HEADS = 8
HEAD_DIM = 64
HALF = HEAD_DIM // 2
B_WIDTH = B_HEADS * HEAD_DIM
IDX_HEADS = 8
IDX_DIM = HEAD_DIM
TOPK_MAX = 256
ROPE_THETA = 10000.0
LN_EPS = 1e-5

LANES = 128
VMEM_LIMIT_BYTES = 56 * 1024 * 1024

SPLITS = (A_WIDTH, A_WIDTH, A_WIDTH, B_WIDTH, HEAD_DIM, HEAD_DIM, B_WIDTH,
          IDX_HEADS * IDX_DIM, IDX_DIM, IDX_HEADS)
D_IN = sum(SPLITS)
_OFF = np.concatenate([[0], np.cumsum(SPLITS)]).astype(np.int64)
U_OFF, V_OFF, ZA_OFF, Q_OFF, K_OFF, VAL_OFF, ZB_OFF, QI_OFF, KI_OFF, W_OFF = (int(o) for o in _OFF[:-1])

C_U, C_V, C_ZA, C_ZB, C_Q, C_QI = 0, 512, 1024, 1536, 2048, 2560
C_KK, C_KI, C_VW = 3072, 3200, 3328
D_PROJ = 3456
W_LANE = HEAD_DIM

Q_TILE = 256
KEY_CHUNK = 512
COUNT_ROWS = 64
NEG_INF = float("-inf")
M_INIT = -1e30

MIN_NORMAL = float(np.finfo(np.float32).tiny)
KEY32_NEG_INF = -(2 ** 31) + 0x007FFFFF
KEY32_MIN_NORMAL = 0x00800000
KEY32_NEG_MIN_NORMAL = -0x00800000 - 1
KEY16_NEG_INF = -0x8000 + 0x007F
KEY16_MIN_NORMAL = 0x0080
KEY16_NEG_MIN_NORMAL = -0x0080 - 1
KEY16_POS_INF = 0x7F80
BF16_SEARCH_STEPS = 15
F32_SEARCH_MAX_STEPS = 20
F32_STEPS_PER_TEST = 2
COUNT_UNKNOWN = 1e9


def _pair_cols(off):
    cols = []
    for p in range(B_HEADS // 2):
        e, o = off + HEAD_DIM * (2 * p), off + HEAD_DIM * (2 * p + 1)
        cols += [np.arange(e, e + HALF), np.arange(o, o + HALF),
                 np.arange(e + HALF, e + HEAD_DIM), np.arange(o + HALF, o + HEAD_DIM)]
    return np.concatenate(cols)


def _dup_cols(off):
    a, b = np.arange(off, off + HALF), np.arange(off + HALF, off + HEAD_DIM)
    return np.concatenate([a, a, b, b])


def _proj_perm():
    zero = D_IN
    vw = np.concatenate([np.arange(VAL_OFF, VAL_OFF + HEAD_DIM), np.arange(W_OFF, W_OFF + IDX_HEADS),
                         np.full((LANES - HEAD_DIM - IDX_HEADS,), zero)])
    perm = np.concatenate([
        np.arange(U_OFF, U_OFF + A_WIDTH), np.arange(V_OFF, V_OFF + A_WIDTH),
        np.arange(ZA_OFF, ZA_OFF + A_WIDTH), np.arange(ZB_OFF, ZB_OFF + B_WIDTH),
        _pair_cols(Q_OFF), _pair_cols(QI_OFF), _dup_cols(K_OFF), _dup_cols(KI_OFF), vw])
    assert perm.shape[0] == D_PROJ
    return perm


def _permute_cols(w, perm):
    runs, start = [], 0
    for j in range(1, len(perm) + 1):
        if j < len(perm):
            a, b = int(perm[j - 1]), int(perm[j])
            if (a == D_IN and b == D_IN) or (a != D_IN and b != D_IN and b == a + 1):
                continue
        runs.append((int(perm[start]), j - start))
        start = j
    parts = []
    for first, n in runs:
        if first == D_IN:
            parts.append(jnp.zeros(w.shape[:-1] + (n,), w.dtype))
        else:
            parts.append(w[..., first:first + n])
    return jnp.concatenate(parts, axis=-1)


def _silu(x):
    return x * (1.0 / (1.0 + jnp.exp(-x)))


def _dot_nt(a, b):
    return lax.dot_general(a, b, (((1,), (1,)), ((), ())), preferred_element_type=jnp.float32)


def _rope_kernel(pos_ref, invf_ref, sgn_ref, cos_ref, sin_ref):
    ang = pos_ref[...].astype(jnp.float32) * invf_ref[...]
    cos_ref[...] = jnp.cos(ang)
    sin_ref[...] = jnp.sin(ang) * sgn_ref[...]


def _rope_tables(positions):
    b, s = positions.shape
    ts = min(s, 1024)
    inv_freq = ROPE_THETA ** (-jnp.arange(0, HEAD_DIM, 2, dtype=jnp.float32) / HEAD_DIM)
    invf = jnp.tile(inv_freq, LANES // HALF)[None, :]
    sgn = jnp.concatenate([-jnp.ones((LANES // 2,), jnp.float32), jnp.ones((LANES // 2,), jnp.float32)])[None, :]
    out = jax.ShapeDtypeStruct((b, s, LANES), jnp.float32)
    return pl.pallas_call(
        _rope_kernel,
        out_shape=(out, out),
        grid=(b, s // ts),
        in_specs=[pl.BlockSpec((None, ts, 1), lambda i, j: (i, j, 0)),
                  pl.BlockSpec((1, LANES), lambda i, j: (0, 0)),
                  pl.BlockSpec((1, LANES), lambda i, j: (0, 0))],
        out_specs=(pl.BlockSpec((None, ts, LANES), lambda i, j: (i, j, 0)),
                   pl.BlockSpec((None, ts, LANES), lambda i, j: (i, j, 0))),
        name="rope_tables",
    )(positions.reshape(b, s, 1), invf, sgn)


def _mod_kernel(c_ref, w_ref, b_ref, o_ref):
    cond = _silu(c_ref[...])
    o_ref[...] = jnp.dot(cond, w_ref[...], preferred_element_type=jnp.float32,
                         precision=lax.Precision.HIGHEST) + b_ref[...]


def _modulation(c, w_ada, b_ada):
    depth, d, d3 = w_ada.shape
    b = c.shape[0]
    return pl.pallas_call(
        _mod_kernel,
        out_shape=jax.ShapeDtypeStruct((depth, b, d3), jnp.float32),
        grid=(depth, d3 // d),
        in_specs=[pl.BlockSpec((b, d), lambda l, j: (0, 0)),
                  pl.BlockSpec((None, d, d), lambda l, j: (l, 0, j)),
                  pl.BlockSpec((None, 1, d), lambda l, j: (l, 0, j))],
        out_specs=pl.BlockSpec((None, b, d), lambda l, j: (l, 0, j)),
        compiler_params=pltpu.CompilerParams(vmem_limit_bytes=VMEM_LIMIT_BYTES),
        name="adaln_mod",
    )(c, w_ada, b_ada.reshape(depth, 1, d3))


def _rope128(x, cos, sin_signed):
    return x * cos + pltpu.roll(x, LANES // 2, 1) * sin_signed


def _in_proj_kernel(x_ref, mod_ref, cos_ref, sin_ref, w_ref, gv_ref, bv_ref, ws_ref, bs_ref,
                    ya_ref, gb_ref, qr_ref, qi_ref, kk_ref, ki_ref, vx_ref, wi_ref):
    tm = x_ref.shape[0]
    x = x_ref[...]
    mu = jnp.mean(x, axis=-1, keepdims=True)
    xc = x - mu
    var = jnp.mean(xc * xc, axis=-1, keepdims=True)
    xn = xc * lax.rsqrt(var + LN_EPS)
    h = xn * (1.0 + mod_ref[1]) + mod_ref[0]
    hb = h.astype(jnp.bfloat16)

    def proj(c0, n):
        return jnp.dot(hb, w_ref[:, c0:c0 + n], preferred_element_type=jnp.float32)

    v = proj(C_V, A_WIDTH)
    vmu = jnp.mean(v, axis=-1, keepdims=True)
    vc = v - vmu
    vvar = jnp.mean(vc * vc, axis=-1, keepdims=True)
    vn = (vc * lax.rsqrt(vvar + LN_EPS) * gv_ref[...] + bv_ref[...]).astype(jnp.bfloat16)
    r_i = lax.broadcasted_iota(jnp.int32, (CHUNK, CHUNK), 0)
    c_i = lax.broadcasted_iota(jnp.int32, (CHUNK, CHUNK), 1)
    wm = [jnp.where(r_i >= c_i, ws_ref[g], 0.0).astype(jnp.bfloat16) for g in range(A_GROUPS)]
    gate_a = _silu(proj(C_ZA, A_WIDTH)) * proj(C_U, A_WIDTH)
    for c in range(tm // CHUNK):
        rows = slice(c * CHUNK, (c + 1) * CHUNK)
        mixed = jnp.concatenate(
            [jnp.dot(wm[g], vn[rows, g * A_GDIM:(g + 1) * A_GDIM], preferred_element_type=jnp.float32)
             for g in range(A_GROUPS)], axis=1) + bs_ref[...]
        ya_ref[rows, :] = (gate_a[rows, :] * mixed).astype(jnp.bfloat16)

    gb_ref[...] = _silu(proj(C_ZB, B_WIDTH)).astype(jnp.bfloat16)
    cos = cos_ref[...]
    sin = sin_ref[...]
    q_scale = HEAD_DIM ** -0.5 * math.log2(math.e)
    qi_scale = IDX_DIM ** -0.5
    q = proj(C_Q, B_WIDTH)
    qi = proj(C_QI, IDX_HEADS * IDX_DIM)
    for p in range(B_WIDTH // LANES):
        ls = slice(p * LANES, (p + 1) * LANES)
        qr_ref[:, ls] = (_rope128(q[:, ls], cos, sin) * q_scale).astype(jnp.bfloat16)
        qi_ref[:, ls] = (_rope128(qi[:, ls], cos, sin) * qi_scale).astype(jnp.bfloat16)
    kk_ref[...] = _rope128(proj(C_KK, LANES), cos, sin).astype(jnp.bfloat16)
    ki_ref[...] = _rope128(proj(C_KI, LANES), cos, sin).astype(jnp.bfloat16)
    vw = proj(C_VW, LANES)
    lane = lax.broadcasted_iota(jnp.int32, vw.shape, 1)
    vx_ref[...] = jnp.where(lane < HEAD_DIM, vw, jnp.where(lane == HEAD_DIM, 1.0, 0.0)).astype(jnp.bfloat16)
    wi_ref[...] = vw * (IDX_HEADS ** -0.5)


def _in_proj(x, mod4, cos, sin, w_p, gv, bv, ws, bs_full, tm):
    b, s, d = x.shape
    row = lambda i, j: (i, j, 0)
    const2 = lambda i, j: (0, 0)
    bf = jnp.bfloat16
    shapes = (
        jax.ShapeDtypeStruct((b, s, A_WIDTH), bf),
        jax.ShapeDtypeStruct((b, s, B_WIDTH), bf),
        jax.ShapeDtypeStruct((b, s, B_WIDTH), bf),
        jax.ShapeDtypeStruct((b, s, B_WIDTH), bf),
        jax.ShapeDtypeStruct((b, s, LANES), bf),
        jax.ShapeDtypeStruct((b, s, LANES), bf),
        jax.ShapeDtypeStruct((b, s, LANES), bf),
        jax.ShapeDtypeStruct((b, s, LANES), jnp.float32),
    )
    wide = pl.BlockSpec((None, tm, B_WIDTH), row)
    narrow = pl.BlockSpec((None, tm, LANES), row)
    return pl.pallas_call(
        _in_proj_kernel,
        out_shape=shapes,
        grid=(b, s // tm),
        in_specs=[pl.BlockSpec((None, tm, d), row),
                  pl.BlockSpec((None, 3, 1, d), lambda i, j: (i, 0, 0, 0)),
                  narrow, narrow,
                  pl.BlockSpec((d, D_PROJ), const2),
                  pl.BlockSpec((1, A_WIDTH), const2),
                  pl.BlockSpec((1, A_WIDTH), const2),
                  pl.BlockSpec((A_GROUPS, CHUNK, CHUNK), lambda i, j: (0, 0, 0)),
                  pl.BlockSpec((CHUNK, A_WIDTH), const2)],
        out_specs=(wide, wide, wide, wide, narrow, narrow, narrow, narrow),
        compiler_params=pltpu.CompilerParams(vmem_limit_bytes=VMEM_LIMIT_BYTES),
        name="in_proj",
    )(x, mod4, cos, sin, w_p, gv, bv, ws, bs_full)


def _key32_to_float(key):
    bits = jnp.where(key >= 0, key, key ^ 0x7FFFFFFF)
    return lax.bitcast_convert_type(bits, jnp.float32)


def _key16_to_float(key):
    bits = jnp.where(key >= 0, key, key ^ 0x7FFF)
    return lax.bitcast_convert_type(jnp.left_shift(bits, 16), jnp.float32)


def _key16_to_key32(key):
    return jnp.left_shift(key, 16) + jnp.where(key < 0, 0xFFFF, 0)


def _attn_kernel(qi_ref, qr_ref, wi_ref, gb_ref, ki_ref, kk_ref, vx_ref, yb_ref,
                 score_ref, sb_ref, rank_ref, qim_ref, qrm_ref, wt_ref, m_ref, acc_ref, *, top_k):
    i = pl.program_id(1)
    tq = Q_TILE
    ck = KEY_CHUNK
    n_chunks = (i * tq) // ck + 1
    lanes_per_chunk = ck // LANES
    f32 = jnp.float32
    bf16 = jnp.bfloat16

    lane = lax.broadcasted_iota(jnp.int32, (tq, LANES), 1)
    even = (lane % HEAD_DIM) < HALF
    q_pos = i * tq + lax.broadcasted_iota(jnp.int32, (1, tq), 1)
    key_iota = lax.broadcasted_iota(jnp.int32, (ck, 1), 0)

    for p in range(B_HEADS // 2):
        ls = slice(p * LANES, (p + 1) * LANES)
        for src, dst in ((qi_ref, qim_ref), (qr_ref, qrm_ref)):
            pair = src[:, ls]
            zero = jnp.zeros_like(pair)
            dst[2 * p * tq:(2 * p + 1) * tq, :] = jnp.where(even, pair, zero)
            dst[(2 * p + 1) * tq:(2 * p + 2) * tq, :] = jnp.where(even, zero, pair)
    wt_ref[...] = wi_ref[...].T

    def chunk_off(c):
        return pl.multiple_of(c * ck, ck)

    def idx_body(c, carry):
        off = chunk_off(c)
        logit_t = _dot_nt(ki_ref[pl.ds(off, ck), :], qim_ref[...])
        acc = None
        for h in range(IDX_HEADS):
            term = wt_ref[W_LANE + h:W_LANE + h + 1, :] * jnp.maximum(logit_t[:, h * tq:(h + 1) * tq], 0.0)
            acc = term if acc is None else acc + term
        score = jnp.where(off + key_iota <= q_pos, acc, NEG_INF)
        score_ref[pl.ds(off, ck), :] = score
        sb_ref[pl.ds(off, ck), :] = score.astype(bf16)
        return carry

    lax.fori_loop(0, n_chunks, idx_body, 0)

    def count_ge(thr):
        def body(c, acc):
            off = chunk_off(c)
            for r in range(0, ck, COUNT_ROWS):
                acc = acc + jnp.where(score_ref[pl.ds(off + r, COUNT_ROWS), :] >= thr, 1.0, 0.0)
            return acc

        acc = lax.fori_loop(0, n_chunks, body, jnp.zeros((COUNT_ROWS, tq), f32))
        return jnp.sum(acc, axis=0, keepdims=True)

    one_b, zero_b = jnp.ones((), bf16), jnp.zeros((), bf16)

    def count_ge_bf16(thr):
        thr_b = jnp.broadcast_to(thr, (COUNT_ROWS, tq)).astype(bf16)

        def body(c, acc):
            off = chunk_off(c)
            for r in range(0, ck, COUNT_ROWS):
                acc = acc + jnp.where(sb_ref[pl.ds(off + r, COUNT_ROWS), :] >= thr_b, one_b, zero_b)
            return acc

        acc = lax.fori_loop(0, n_chunks, body, jnp.zeros((COUNT_ROWS, tq), bf16))
        return jnp.sum(acc.astype(f32), axis=0, keepdims=True)

    kf = float(top_k)
    n_nonneg = count_ge_bf16(jnp.zeros((1, tq), f32))
    n_pos = count_ge_bf16(jnp.full((1, tq), MIN_NORMAL, f32))
    is_pos = n_pos >= kf
    is_neg = n_nonneg < kf
    is_zero = jnp.logical_not(jnp.logical_or(is_pos, is_neg))

    lo16 = jnp.where(is_pos, KEY16_MIN_NORMAL, jnp.where(is_neg, KEY16_NEG_INF, 0)).astype(jnp.int32)
    hi16 = jnp.where(is_pos, KEY16_POS_INF + 1, jnp.where(is_neg, KEY16_NEG_MIN_NORMAL + 1, 1)).astype(jnp.int32)

    def bf16_step(_, carry):
        lo, hi = carry
        mid = lo + jnp.right_shift(hi - lo, 1)
        ge = count_ge_bf16(_key16_to_float(mid)) >= kf
        return jnp.where(ge, mid, lo), jnp.where(ge, hi, mid)

    lo16, hi16 = lax.fori_loop(0, BF16_SEARCH_STEPS, bf16_step, (lo16, hi16))

    lo32 = jnp.maximum(_key16_to_key32(lo16) - 0x8001, KEY32_NEG_INF)
    hi32 = _key16_to_key32(lo16 + 1)
    lo32 = jnp.where(is_pos, jnp.maximum(lo32, KEY32_MIN_NORMAL), lo32)
    hi32 = jnp.where(is_neg, jnp.minimum(hi32, KEY32_NEG_MIN_NORMAL + 1), hi32)
    lo32 = jnp.where(is_zero, 0, lo32)
    hi32 = jnp.where(is_zero, 1, hi32)
    n_lo = jnp.where(is_zero, n_nonneg, COUNT_UNKNOWN)

    def f32_active(lo, hi, n_at_lo):
        return jnp.logical_and(n_at_lo != kf, hi - lo > 1)

    def any_true(mask):
        return jnp.max(jnp.where(mask, 1.0, 0.0)) > 0.0

    def f32_cond(carry):
        step, _, _, _, more = carry
        return jnp.logical_and(more, step < F32_SEARCH_MAX_STEPS)

    def f32_halve(lo, hi, n_at_lo):
        active = f32_active(lo, hi, n_at_lo)
        mid = lo + jnp.right_shift(hi - lo, 1)
        n_mid = count_ge(_key32_to_float(mid))
        up = jnp.logical_and(active, n_mid >= kf)
        down = jnp.logical_and(active, n_mid < kf)
        return jnp.where(up, mid, lo), jnp.where(down, mid, hi), jnp.where(up, n_mid, n_at_lo)

    def f32_step(carry):
        step, lo, hi, n_at_lo, _ = carry
        for _ in range(F32_STEPS_PER_TEST):
            lo, hi, n_at_lo = f32_halve(lo, hi, n_at_lo)
        return step + F32_STEPS_PER_TEST, lo, hi, n_at_lo, any_true(f32_active(lo, hi, n_at_lo))

    _, lo32, hi32, n_lo, _ = lax.while_loop(
        f32_cond, f32_step, (jnp.int32(0), lo32, hi32, n_lo, any_true(f32_active(lo32, hi32, n_lo))))

    thr = jnp.where(is_zero, 0.0, _key32_to_float(lo32))
    thr_next = jnp.where(is_zero, MIN_NORMAL, _key32_to_float(lo32 + 1))
    maybe_tie = n_lo > kf

    def rank_ties():
        r_i = lax.broadcasted_iota(jnp.int32, (CHUNK, CHUNK), 0)
        c_i = lax.broadcasted_iota(jnp.int32, (CHUNK, CHUNK), 1)
        tri = jnp.where(c_i <= r_i, 1.0, 0.0).astype(bf16)

        def body(c, carry):
            above, ties_before = carry
            off = chunk_off(c)
            for r in range(0, ck, CHUNK):
                s = score_ref[pl.ds(off + r, CHUNK), :]
                is_above = jnp.where(s >= thr_next, 1.0, 0.0)
                tie = jnp.where(s >= thr, 1.0 - is_above, 0.0).astype(bf16)
                for r2 in range(0, CHUNK, COUNT_ROWS):
                    above = above + is_above[r2:r2 + COUNT_ROWS]
                rank = jnp.dot(tri, tie, preferred_element_type=f32) + ties_before
                rank_ref[pl.ds(off + r, CHUNK), :] = rank
                ties_before = rank[CHUNK - 1:CHUNK, :]
            return above, ties_before

        above, _ = lax.fori_loop(0, n_chunks, body, (jnp.zeros((COUNT_ROWS, tq), f32), jnp.zeros((1, tq), f32)))
        return kf - jnp.sum(above, axis=0, keepdims=True)

    def no_ties():
        def body(c, carry):
            rank_ref[pl.ds(chunk_off(c), ck), :] = jnp.zeros((ck, tq), f32)
            return carry

        lax.fori_loop(0, n_chunks, body, 0)
        return jnp.full((1, tq), COUNT_UNKNOWN, f32)

    ties_taken = lax.cond(any_true(maybe_tie), rank_ties, no_ties)

    m_ref[...] = jnp.full(m_ref.shape, M_INIT, f32)
    acc_ref[...] = jnp.zeros(acc_ref.shape, f32)

    def att_body(c, carry):
        off = chunk_off(c)
        s_idx = score_ref[pl.ds(off, ck), :]
        key_pos = off + key_iota
        keep_tie = jnp.where(rank_ref[pl.ds(off, ck), :] <= ties_taken, 0.0, NEG_INF)
        bias_t = jnp.where(s_idx >= thr, jnp.where(s_idx >= thr_next, 0.0, keep_tie), NEG_INF)
        bias_t = jnp.where(key_pos <= q_pos, bias_t, NEG_INF)
        bias = bias_t.T
        s = _dot_nt(qrm_ref[...], kk_ref[pl.ds(off, ck), :])
        s = (s.reshape(B_HEADS, tq, ck) + bias[None]).reshape(B_HEADS * tq, ck)
        m_old = m_ref[...]
        m_new = jnp.maximum(m_old, jnp.max(s, axis=1, keepdims=True))
        p = jnp.concatenate(
            [jnp.exp2(s[:, j * LANES:(j + 1) * LANES] - m_new) for j in range(lanes_per_chunk)],
            axis=1).astype(jnp.bfloat16)
        acc_ref[...] = acc_ref[...] * jnp.exp2(m_old - m_new) + jnp.dot(
            p, vx_ref[pl.ds(off, ck), :], preferred_element_type=f32)
        m_ref[...] = m_new
        return carry

    lax.fori_loop(0, n_chunks, att_body, 0)

    for p in range(B_HEADS // 2):
        outs = []
        for h in (2 * p, 2 * p + 1):
            a = acc_ref[h * tq:(h + 1) * tq, :]
            denom = jnp.sum(jnp.where(lane == HEAD_DIM, a, 0.0), axis=1, keepdims=True)
            outs.append(a / denom)
        pair = jnp.where(lane < HEAD_DIM, outs[0], pltpu.roll(outs[1], HEAD_DIM, 1))
        ls = slice(p * LANES, (p + 1) * LANES)
        yb_ref[:, ls] = (pair * gb_ref[:, ls].astype(f32)).astype(jnp.bfloat16)


def _sparse_attn(qi, qr, wi, gb, ki, kk, vx):
    b, s, _ = qr.shape
    top_k = min(TOPK_MAX, s // 4)
    blk = lambda i, j: (i, j, 0)
    full = lambda i, j: (i, 0, 0)
    wide = pl.BlockSpec((None, Q_TILE, B_WIDTH), blk)
    keys = pl.BlockSpec((None, s, LANES), full)
    return pl.pallas_call(
        functools.partial(_attn_kernel, top_k=top_k),
        out_shape=jax.ShapeDtypeStruct((b, s, B_WIDTH), jnp.bfloat16),
        grid=(b, s // Q_TILE),
        in_specs=[wide, wide, pl.BlockSpec((None, Q_TILE, LANES), blk), wide, keys, keys, keys],
        out_specs=wide,
        scratch_shapes=[pltpu.VMEM((s, Q_TILE), jnp.float32),
                        pltpu.VMEM((s, Q_TILE), jnp.bfloat16),
                        pltpu.VMEM((s, Q_TILE), jnp.float32),
                        pltpu.VMEM((IDX_HEADS * Q_TILE, LANES), jnp.bfloat16),
                        pltpu.VMEM((B_HEADS * Q_TILE, LANES), jnp.bfloat16),
                        pltpu.VMEM((LANES, Q_TILE), jnp.float32),
                        pltpu.VMEM((B_HEADS * Q_TILE, LANES), jnp.float32),
                        pltpu.VMEM((B_HEADS * Q_TILE, LANES), jnp.float32)],
        compiler_params=pltpu.CompilerParams(vmem_limit_bytes=VMEM_LIMIT_BYTES),
        name="sparse_attn",
    )(qi, qr, wi, gb, ki, kk, vx)


def _out_proj_kernel(x_ref, ya_ref, yb_ref, w_ref, mod_ref, g_ref, b_ref, o_ref, *, alpha):
    y = (jnp.dot(ya_ref[...], w_ref[:A_WIDTH, :], preferred_element_type=jnp.float32)
         + jnp.dot(yb_ref[...], w_ref[A_WIDTH:, :], preferred_element_type=jnp.float32))
    r = alpha * x_ref[...] + mod_ref[2] * y
    mu = jnp.mean(r, axis=-1, keepdims=True)
    rc = r - mu
    var = jnp.mean(rc * rc, axis=-1, keepdims=True)
    o_ref[...] = rc * lax.rsqrt(var + LN_EPS) * g_ref[...] + b_ref[...]


def _out_proj(x, ya, yb, w_out, mod4, ln_g, ln_b, alpha, tm):
    b, s, d = x.shape
    row = lambda i, j: (i, j, 0)
    const2 = lambda i, j: (0, 0)
    return pl.pallas_call(
        functools.partial(_out_proj_kernel, alpha=alpha),
        out_shape=jax.ShapeDtypeStruct((b, s, d), jnp.float32),
        grid=(b, s // tm),
        in_specs=[pl.BlockSpec((None, tm, d), row),
                  pl.BlockSpec((None, tm, A_WIDTH), row),
                  pl.BlockSpec((None, tm, B_WIDTH), row),
                  pl.BlockSpec((A_WIDTH + B_WIDTH, d), const2),
                  pl.BlockSpec((None, 3, 1, d), lambda i, j: (i, 0, 0, 0)),
                  pl.BlockSpec((1, d), const2),
                  pl.BlockSpec((1, d), const2)],
        out_specs=pl.BlockSpec((None, tm, d), row),
        compiler_params=pltpu.CompilerParams(vmem_limit_bytes=VMEM_LIMIT_BYTES),
        name="out_proj",
    )(x, ya, yb, w_out, mod4, ln_g, ln_b)


def kernel(x, c, positions, w_ada, b_ada, w_in, v_norm_g, v_norm_b, w_spatial, b_spatial, w_out, ln_g, ln_b):
    depth, d, _ = w_in.shape
    b, s, _ = x.shape
    assert d == D_MODEL and s % KEY_CHUNK == 0
    tm = min(512, s)
    alpha = (2.0 * depth) ** 0.25

    cos, sin = _rope_tables(positions)
    mod = _modulation(c, w_ada, b_ada)
    mod4 = mod.reshape(depth, b, 3, 1, d)

    w_p = _permute_cols(w_in.astype(jnp.bfloat16), _proj_perm())
    w_o = w_out.astype(jnp.bfloat16)
    bs_full = jnp.repeat(jnp.swapaxes(b_spatial, 1, 2), A_GDIM, axis=2)

    for l in range(depth):
        ya, gb, qr, qi, kk, ki, vx, wi = _in_proj(
            x, mod4[l], cos, sin, w_p[l], v_norm_g[l][None, :], v_norm_b[l][None, :],
            w_spatial[l], bs_full[l], tm)
        yb = _sparse_attn(qi, qr, wi, gb, ki, kk, vx)
        x = _out_proj(x, ya, yb, w_o[l], mod4[l], ln_g[l][None, :], ln_b[l][None, :], alpha, tm)
    return x
```

```python
import functools
import math

import numpy as np
import jax
import jax.numpy as jnp
from jax import lax
from jax.experimental import pallas as pl
from jax.experimental.pallas import tpu as pltpu

D_MODEL = 1024
A_WIDTH = D_MODEL // 2
A_GROUPS = 8
A_GDIM = A_WIDTH // A_GROUPS
CHUNK = 128
B_HEADS = 8
HEAD_DIM = 64
HALF = HEAD_DIM // 2
B_WIDTH = B_HEADS * HEAD_DIM
IDX_HEADS = 8
IDX_DIM = HEAD_DIM
TOPK_MAX = 256
ROPE_THETA = 10000.0
LN_EPS = 1e-5

LANES = 128
VMEM_LIMIT_BYTES = 56 * 1024 * 1024

SPLITS = (A_WIDTH, A_WIDTH, A_WIDTH, B_WIDTH, HEAD_DIM, HEAD_DIM, B_WIDTH,
          IDX_HEADS * IDX_DIM, IDX_DIM, IDX_HEADS)
D_IN = sum(SPLITS)
_OFF = np.concatenate([[0], np.cumsum(SPLITS)]).astype(np.int64)
U_OFF, V_OFF, ZA_OFF, Q_OFF, K_OFF, VAL_OFF, ZB_OFF, QI_OFF, KI_OFF, W_OFF = (int(o) for o in _OFF[:-1])

C_U, C_V, C_ZA, C_ZB, C_Q, C_QI = 0, 512, 1024, 1536, 2048, 2560
C_KK, C_KI, C_VW = 3072, 3200, 3328
D_PROJ = 3456
W_LANE = HEAD_DIM

Q_TILE = 512
KEY_CHUNK = Q_TILE
COUNT_ROWS = 64
NEG_INF = float("-inf")
M_INIT = -1e30

MIN_NORMAL = float(np.finfo(np.float32).tiny)
KEY32_NEG_INF = -(2 ** 31) + 0x007FFFFF
KEY32_MIN_NORMAL = 0x00800000
KEY32_NEG_MIN_NORMAL = -0x00800000 - 1
KEY16_NEG_INF = -0x8000 + 0x007F
KEY16_MIN_NORMAL = 0x0080
KEY16_NEG_MIN_NORMAL = -0x0080 - 1
KEY16_POS_INF = 0x7F80
BF16_SEARCH_STEPS = 15
F32_SEARCH_MAX_STEPS = 20
F32_STEPS_PER_TEST = 2
COUNT_UNKNOWN = 1e9
QV_PROBE, QV_THR, QV_NEXT, QV_TAKEN, QV_ROWS = 0, 1, 2, 3, 8


def _pair_cols(off):
    cols = []
    for p in range(B_HEADS // 2):
        e, o = off + HEAD_DIM * (2 * p), off + HEAD_DIM * (2 * p + 1)
        cols += [np.arange(e, e + HALF), np.arange(o, o + HALF),
                 np.arange(e + HALF, e + HEAD_DIM), np.arange(o + HALF, o + HEAD_DIM)]
    return np.concatenate(cols)


def _dup_cols(off):
    a, b = np.arange(off, off + HALF), np.arange(off + HALF, off + HEAD_DIM)
    return np.concatenate([a, a, b, b])


def _proj_perm():
    zero = D_IN
    vw = np.concatenate([np.arange(VAL_OFF, VAL_OFF + HEAD_DIM), np.arange(W_OFF, W_OFF + IDX_HEADS),
                         np.full((LANES - HEAD_DIM - IDX_HEADS,), zero)])
    perm = np.concatenate([
        np.arange(U_OFF, U_OFF + A_WIDTH), np.arange(V_OFF, V_OFF + A_WIDTH),
        np.arange(ZA_OFF, ZA_OFF + A_WIDTH), np.arange(ZB_OFF, ZB_OFF + B_WIDTH),
        _pair_cols(Q_OFF), _pair_cols(QI_OFF), _dup_cols(K_OFF), _dup_cols(KI_OFF), vw])
    assert perm.shape[0] == D_PROJ
    return perm


def _permute_cols(w, perm):
    runs, start = [], 0
    for j in range(1, len(perm) + 1):
        if j < len(perm):
            a, b = int(perm[j - 1]), int(perm[j])
            if (a == D_IN and b == D_IN) or (a != D_IN and b != D_IN and b == a + 1):
                continue
        runs.append((int(perm[start]), j - start))
        start = j
    parts = []
    for first, n in runs:
        if first == D_IN:
            parts.append(jnp.zeros(w.shape[:-1] + (n,), w.dtype))
        else:
            parts.append(w[..., first:first + n])
    return jnp.concatenate(parts, axis=-1)


def _silu(x):
    return x * (1.0 / (1.0 + jnp.exp(-x)))


def _dot_nt(a, b):
    return lax.dot_general(a, b, (((1,), (1,)), ((), ())), preferred_element_type=jnp.float32)


def _rope_kernel(pos_ref, invf_ref, sgn_ref, cos_ref, sin_ref):
    ang = pos_ref[...].astype(jnp.float32) * invf_ref[...]
    cos_ref[...] = jnp.cos(ang)
    sin_ref[...] = jnp.sin(ang) * sgn_ref[...]


def _rope_tables(positions):
    b, s = positions.shape
    ts = min(s, 1024)
    inv_freq = ROPE_THETA ** (-jnp.arange(0, HEAD_DIM, 2, dtype=jnp.float32) / HEAD_DIM)
    invf = jnp.tile(inv_freq, LANES // HALF)[None, :]
    sgn = jnp.concatenate([-jnp.ones((LANES // 2,), jnp.float32), jnp.ones((LANES // 2,), jnp.float32)])[None, :]
    out = jax.ShapeDtypeStruct((b, s, LANES), jnp.float32)
    return pl.pallas_call(
        _rope_kernel,
        out_shape=(out, out),
        grid=(b, s // ts),
        in_specs=[pl.BlockSpec((None, ts, 1), lambda i, j: (i, j, 0)),
                  pl.BlockSpec((1, LANES), lambda i, j: (0, 0)),
                  pl.BlockSpec((1, LANES), lambda i, j: (0, 0))],
        out_specs=(pl.BlockSpec((None, ts, LANES), lambda i, j: (i, j, 0)),
                   pl.BlockSpec((None, ts, LANES), lambda i, j: (i, j, 0))),
        name="rope_tables",
    )(positions.reshape(b, s, 1), invf, sgn)


def _mod_kernel(c_ref, w_ref, b_ref, o_ref):
    cond = _silu(c_ref[...])
    o_ref[...] = jnp.dot(cond, w_ref[...], preferred_element_type=jnp.float32,
                         precision=lax.Precision.HIGHEST) + b_ref[...]


def _modulation(c, w_ada, b_ada):
    depth, d, d3 = w_ada.shape
    b = c.shape[0]
    return pl.pallas_call(
        _mod_kernel,
        out_shape=jax.ShapeDtypeStruct((depth, b, d3), jnp.float32),
        grid=(depth, d3 // d),
        in_specs=[pl.BlockSpec((b, d), lambda l, j: (0, 0)),
                  pl.BlockSpec((None, d, d), lambda l, j: (l, 0, j)),
                  pl.BlockSpec((None, 1, d), lambda l, j: (l, 0, j))],
        out_specs=pl.BlockSpec((None, b, d), lambda l, j: (l, 0, j)),
        compiler_params=pltpu.CompilerParams(vmem_limit_bytes=VMEM_LIMIT_BYTES),
        name="adaln_mod",
    )(c, w_ada, b_ada.reshape(depth, 1, d3))


def _rope128(x, cos, sin_signed):
    return x * cos + pltpu.roll(x, LANES // 2, 1) * sin_signed


def _in_proj_kernel(x_ref, mod_ref, cos_ref, sin_ref, w_ref, gv_ref, bv_ref, ws_ref, bs_ref,
                    ya_ref, gb_ref, qr_ref, qi_ref, kk_ref, ki_ref, vx_ref, wi_ref):
    tm = x_ref.shape[0]
    x = x_ref[...]
    mu = jnp.mean(x, axis=-1, keepdims=True)
    xc = x - mu
    var = jnp.mean(xc * xc, axis=-1, keepdims=True)
    xn = xc * lax.rsqrt(var + LN_EPS)
    h = xn * (1.0 + mod_ref[1]) + mod_ref[0]
    hb = h.astype(jnp.bfloat16)

    def proj(c0, n):
        return jnp.dot(hb, w_ref[:, c0:c0 + n], preferred_element_type=jnp.float32)

    v = proj(C_V, A_WIDTH)
    vmu = jnp.mean(v, axis=-1, keepdims=True)
    vc = v - vmu
    vvar = jnp.mean(vc * vc, axis=-1, keepdims=True)
    vn = (vc * lax.rsqrt(vvar + LN_EPS) * gv_ref[...] + bv_ref[...]).astype(jnp.bfloat16)
    r_i = lax.broadcasted_iota(jnp.int32, (CHUNK, CHUNK), 0)
    c_i = lax.broadcasted_iota(jnp.int32, (CHUNK, CHUNK), 1)
    wm = [jnp.where(r_i >= c_i, ws_ref[g], 0.0).astype(jnp.bfloat16) for g in range(A_GROUPS)]
    gate_a = _silu(proj(C_ZA, A_WIDTH)) * proj(C_U, A_WIDTH)
    for c in range(tm // CHUNK):
        rows = slice(c * CHUNK, (c + 1) * CHUNK)
        mixed = jnp.concatenate(
            [jnp.dot(wm[g], vn[rows, g * A_GDIM:(g + 1) * A_GDIM], preferred_element_type=jnp.float32)
             for g in range(A_GROUPS)], axis=1) + bs_ref[...]
        ya_ref[rows, :] = (gate_a[rows, :] * mixed).astype(jnp.bfloat16)

    gb_ref[...] = _silu(proj(C_ZB, B_WIDTH)).astype(jnp.bfloat16)
    cos = cos_ref[...]
    sin = sin_ref[...]
    q_scale = HEAD_DIM ** -0.5 * math.log2(math.e)
    qi_scale = IDX_DIM ** -0.5
    q = proj(C_Q, B_WIDTH)
    qi = proj(C_QI, IDX_HEADS * IDX_DIM)
    for p in range(B_WIDTH // LANES):
        ls = slice(p * LANES, (p + 1) * LANES)
        qr_ref[:, ls] = (_rope128(q[:, ls], cos, sin) * q_scale).astype(jnp.bfloat16)
        qi_ref[:, ls] = (_rope128(qi[:, ls], cos, sin) * qi_scale).astype(jnp.bfloat16)
    kk_ref[...] = _rope128(proj(C_KK, LANES), cos, sin).astype(jnp.bfloat16)
    ki_ref[...] = _rope128(proj(C_KI, LANES), cos, sin).astype(jnp.bfloat16)
    vw = proj(C_VW, LANES)
    lane = lax.broadcasted_iota(jnp.int32, vw.shape, 1)
    vx_ref[...] = jnp.where(lane < HEAD_DIM, vw, 1.0).astype(jnp.bfloat16)
    wi_ref[...] = vw * (IDX_HEADS ** -0.5)


def _in_proj(x, mod4, cos, sin, w_p, gv, bv, ws, bs_full, tm):
    b, s, d = x.shape
    row = lambda i, j: (i, j, 0)
    const2 = lambda i, j: (0, 0)
    bf = jnp.bfloat16
    shapes = (
        jax.ShapeDtypeStruct((b, s, A_WIDTH), bf),
        jax.ShapeDtypeStruct((b, s, B_WIDTH), bf),
        jax.ShapeDtypeStruct((b, s, B_WIDTH), bf),
        jax.ShapeDtypeStruct((b, s, B_WIDTH), bf),
        jax.ShapeDtypeStruct((b, s, LANES), bf),
        jax.ShapeDtypeStruct((b, s, LANES), bf),
        jax.ShapeDtypeStruct((b, s, LANES), bf),
        jax.ShapeDtypeStruct((b, s, LANES), jnp.float32),
    )
    wide = pl.BlockSpec((None, tm, B_WIDTH), row)
    narrow = pl.BlockSpec((None, tm, LANES), row)
    return pl.pallas_call(
        _in_proj_kernel,
        out_shape=shapes,
        grid=(b, s // tm),
        in_specs=[pl.BlockSpec((None, tm, d), row),
                  pl.BlockSpec((None, 3, 1, d), lambda i, j: (i, 0, 0, 0)),
                  narrow, narrow,
                  pl.BlockSpec((d, D_PROJ), const2),
                  pl.BlockSpec((1, A_WIDTH), const2),
                  pl.BlockSpec((1, A_WIDTH), const2),
                  pl.BlockSpec((A_GROUPS, CHUNK, CHUNK), lambda i, j: (0, 0, 0)),
                  pl.BlockSpec((CHUNK, A_WIDTH), const2)],
        out_specs=(wide, wide, wide, wide, narrow, narrow, narrow, narrow),
        compiler_params=pltpu.CompilerParams(vmem_limit_bytes=VMEM_LIMIT_BYTES),
        name="in_proj",
    )(x, mod4, cos, sin, w_p, gv, bv, ws, bs_full)


def _key32_to_float(key):
    bits = jnp.where(key >= 0, key, key ^ 0x7FFFFFFF)
    return lax.bitcast_convert_type(bits, jnp.float32)


def _key16_to_float(key):
    bits = jnp.where(key >= 0, key, key ^ 0x7FFF)
    return lax.bitcast_convert_type(jnp.left_shift(bits, 16), jnp.float32)


def _key16_to_key32(key):
    return jnp.left_shift(key, 16) + jnp.where(key < 0, 0xFFFF, 0)


def _attn_kernel(qi_ref, qr_ref, wi_ref, gb_ref, ki_ref, kk_ref, vx_ref, yb_ref,
                 score_ref, sb_ref, rank_ref, qim_ref, qrm_ref, wt_ref, qv_ref, m_ref, acc_ref, *, top_k):
    i = pl.program_id(1)
    tq = Q_TILE
    ck = KEY_CHUNK
    hq = tq // 2
    hrows = B_HEADS * hq
    n_full = i
    n_chunks = i + 1
    diag = pl.multiple_of(i * ck, ck)
    f32 = jnp.float32
    bf16 = jnp.bfloat16

    lane = lax.broadcasted_iota(jnp.int32, (hq, LANES), 1)
    even = (lane % HEAD_DIM) < HALF

    def query_positions(halves):
        n_q = len(halves) * hq
        return i * tq + halves[0] * hq + lax.broadcasted_iota(jnp.int32, (1, n_q), 1)

    for half in range(2):
        qs = slice(half * hq, (half + 1) * hq)
        for p in range(B_HEADS // 2):
            ls = slice(p * LANES, (p + 1) * LANES)
            base = half * hrows + 2 * p * hq
            for src, dst in ((qi_ref, qim_ref), (qr_ref, qrm_ref)):
                pair = src[qs, ls]
                zero = jnp.zeros_like(pair)
                dst[base:base + hq, :] = jnp.where(even, pair, zero)
                dst[base + hq:base + 2 * hq, :] = jnp.where(even, zero, pair)
    wt_ref[...] = wi_ref[...].T

    def chunk_off(c):
        return pl.multiple_of(c * ck, ck)

    def half_rows(halves):
        return slice(halves[0] * hrows, (halves[-1] + 1) * hrows)

    def half_lanes(halves):
        return slice(halves[0] * hq, (halves[-1] + 1) * hq)

    def key_positions(key_off, n_k):
        return key_off + lax.broadcasted_iota(jnp.int32, (n_k, 1), 0)

    def idx_block(key_off, n_k, halves, causal):
        logit_t = _dot_nt(ki_ref[pl.ds(key_off, n_k), :], qim_ref[half_rows(halves), :])
        parts = []
        for j, half in enumerate(halves):
            acc = None
            for h in range(IDX_HEADS):
                col = (j * IDX_HEADS + h) * hq
                w_row = wt_ref[W_LANE + h:W_LANE + h + 1, half * hq:(half + 1) * hq]
                term = w_row * jnp.maximum(logit_t[:, col:col + hq], 0.0)
                acc = term if acc is None else acc + term
            parts.append(acc)
        score = parts[0] if len(parts) == 1 else jnp.concatenate(parts, axis=1)
        lanes = half_lanes(halves)
        if causal:
            score = jnp.where(key_positions(key_off, n_k) <= query_positions(halves), score, NEG_INF)
        score_ref[pl.ds(key_off, n_k), lanes] = score
        sb_ref[pl.ds(key_off, n_k), lanes] = score.astype(bf16)

    def idx_body(c, carry):
        idx_block(chunk_off(c), ck, (0, 1), False)
        return carry

    lax.fori_loop(0, n_full, idx_body, 0)
    idx_block(diag, hq, (0,), True)
    idx_block(diag, ck, (1,), True)
    dead_off = pl.multiple_of(diag + hq, hq)
    score_ref[pl.ds(dead_off, ck - hq), 0:hq] = jnp.full((ck - hq, hq), NEG_INF, f32)
    sb_ref[pl.ds(dead_off, ck - hq), 0:hq] = jnp.full((ck - hq, hq), NEG_INF, bf16)

    def count_ge_in(ref, thr, one, zero):
        qv_ref[QV_PROBE:QV_PROBE + 1, :] = thr
        dtype = ref.dtype
        thr_all = jnp.broadcast_to(qv_ref[QV_PROBE:QV_PROBE + 1, :], (COUNT_ROWS, tq)).astype(dtype)
        thr_hi = jnp.broadcast_to(qv_ref[QV_PROBE:QV_PROBE + 1, hq:], (COUNT_ROWS, hq)).astype(dtype)

        def body(c, acc):
            off = chunk_off(c)
            for r in range(0, ck, COUNT_ROWS):
                acc = acc + jnp.where(ref[pl.ds(off + r, COUNT_ROWS), :] >= thr_all, one, zero)
            return acc

        acc = lax.fori_loop(0, n_full, body, jnp.zeros((COUNT_ROWS, tq), dtype))
        for r in range(0, hq, COUNT_ROWS):
            acc = acc + jnp.where(ref[pl.ds(diag + r, COUNT_ROWS), :] >= thr_all, one, zero)
        acc_hi = acc[:, hq:]
        for r in range(hq, ck, COUNT_ROWS):
            acc_hi = acc_hi + jnp.where(ref[pl.ds(diag + r, COUNT_ROWS), hq:] >= thr_hi, one, zero)
        acc = jnp.concatenate([acc[:, :hq], acc_hi], axis=1)
        return jnp.sum(acc.astype(f32), axis=0, keepdims=True)

    def count_ge(thr):
        return count_ge_in(score_ref, thr, 1.0, 0.0)

    def count_ge_bf16(thr):
        return count_ge_in(sb_ref, thr, jnp.ones((), bf16), jnp.zeros((), bf16))

    kf = float(top_k)
    n_nonneg = count_ge_bf16(jnp.zeros((1, tq), f32))
    n_pos = count_ge_bf16(jnp.full((1, tq), MIN_NORMAL, f32))
    is_pos = n_pos >= kf
    is_neg = n_nonneg < kf
    is_zero = jnp.logical_not(jnp.logical_or(is_pos, is_neg))

    lo16 = jnp.where(is_pos, KEY16_MIN_NORMAL, jnp.where(is_neg, KEY16_NEG_INF, 0)).astype(jnp.int32)
    hi16 = jnp.where(is_pos, KEY16_POS_INF + 1, jnp.where(is_neg, KEY16_NEG_MIN_NORMAL + 1, 1)).astype(jnp.int32)

    def bf16_step(_, carry):
        lo, hi = carry
        mid = lo + jnp.right_shift(hi - lo, 1)
        ge = count_ge_bf16(_key16_to_float(mid)) >= kf
        return jnp.where(ge, mid, lo), jnp.where(ge, hi, mid)

    lo16, hi16 = lax.fori_loop(0, BF16_SEARCH_STEPS, bf16_step, (lo16, hi16))

    lo32 = jnp.maximum(_key16_to_key32(lo16) - 0x8001, KEY32_NEG_INF)
    hi32 = _key16_to_key32(lo16 + 1)
    lo32 = jnp.where(is_pos, jnp.maximum(lo32, KEY32_MIN_NORMAL), lo32)
    hi32 = jnp.where(is_neg, jnp.minimum(hi32, KEY32_NEG_MIN_NORMAL + 1), hi32)
    lo32 = jnp.where(is_zero, 0, lo32)
    hi32 = jnp.where(is_zero, 1, hi32)
    n_lo = jnp.where(is_zero, n_nonneg, COUNT_UNKNOWN)

    def f32_active(lo, hi, n_at_lo):
        return jnp.logical_and(n_at_lo != kf, hi - lo > 1)

    def any_true(mask):
        return jnp.max(jnp.where(mask, 1.0, 0.0)) > 0.0

    def f32_cond(carry):
        step, _, _, _, more = carry
        return jnp.logical_and(more, step < F32_SEARCH_MAX_STEPS)

    def f32_halve(lo, hi, n_at_lo):
        active = f32_active(lo, hi, n_at_lo)
        mid = lo + jnp.right_shift(hi - lo, 1)
        n_mid = count_ge(_key32_to_float(mid))
        up = jnp.logical_and(active, n_mid >= kf)
        down = jnp.logical_and(active, n_mid < kf)
        return jnp.where(up, mid, lo), jnp.where(down, mid, hi), jnp.where(up, n_mid, n_at_lo)

    def f32_step(carry):
        step, lo, hi, n_at_lo, _ = carry
        for _ in range(F32_STEPS_PER_TEST):
            lo, hi, n_at_lo = f32_halve(lo, hi, n_at_lo)
        return step + F32_STEPS_PER_TEST, lo, hi, n_at_lo, any_true(f32_active(lo, hi, n_at_lo))

    _, lo32, hi32, n_lo, _ = lax.while_loop(
        f32_cond, f32_step, (jnp.int32(0), lo32, hi32, n_lo, any_true(f32_active(lo32, hi32, n_lo))))

    thr = jnp.where(is_zero, 0.0, _key32_to_float(lo32))
    thr_next = jnp.where(is_zero, MIN_NORMAL, _key32_to_float(lo32 + 1))
    maybe_tie = n_lo > kf

    def rank_ties():
        r_i = lax.broadcasted_iota(jnp.int32, (CHUNK, CHUNK), 0)
        c_i = lax.broadcasted_iota(jnp.int32, (CHUNK, CHUNK), 1)
        tri = jnp.where(c_i <= r_i, 1.0, 0.0).astype(bf16)

        def body(c, carry):
            above, ties_before = carry
            off = chunk_off(c)
            for r in range(0, ck, CHUNK):
                s = score_ref[pl.ds(off + r, CHUNK), :]
                is_above = jnp.where(s >= thr_next, 1.0, 0.0)
                tie = jnp.where(s >= thr, 1.0 - is_above, 0.0).astype(bf16)
                for r2 in range(0, CHUNK, COUNT_ROWS):
                    above = above + is_above[r2:r2 + COUNT_ROWS]
                rank = jnp.dot(tri, tie, preferred_element_type=f32) + ties_before
                rank_ref[pl.ds(off + r, CHUNK), :] = rank
                ties_before = rank[CHUNK - 1:CHUNK, :]
            return above, ties_before

        above, _ = lax.fori_loop(0, n_chunks, body, (jnp.zeros((COUNT_ROWS, tq), f32), jnp.zeros((1, tq), f32)))
        return kf - jnp.sum(above, axis=0, keepdims=True)

    def no_ties():
        def body(c, carry):
            rank_ref[pl.ds(chunk_off(c), ck), :] = jnp.zeros((ck, tq), f32)
            return carry

        lax.fori_loop(0, n_chunks, body, 0)
        return jnp.full((1, tq), COUNT_UNKNOWN, f32)

    ties_taken = lax.cond(any_true(maybe_tie), rank_ties, no_ties)
    qv_ref[QV_THR:QV_THR + 1, :] = thr
    qv_ref[QV_NEXT:QV_NEXT + 1, :] = thr_next
    qv_ref[QV_TAKEN:QV_TAKEN + 1, :] = ties_taken

    m_ref[...] = jnp.full(m_ref.shape, M_INIT, f32)
    acc_ref[...] = jnp.zeros(acc_ref.shape, f32)

    def att_block(key_off, n_k, halves, causal):
        rows, lanes, nh = half_rows(halves), half_lanes(halves), len(halves)
        s_idx = score_ref[pl.ds(key_off, n_k), lanes]
        if nh == 2:
            thr_q, next_q, taken_q = thr, thr_next, ties_taken
        else:
            thr_q = qv_ref[QV_THR:QV_THR + 1, lanes]
            next_q = qv_ref[QV_NEXT:QV_NEXT + 1, lanes]
            taken_q = qv_ref[QV_TAKEN:QV_TAKEN + 1, lanes]
        keep_tie = jnp.where(rank_ref[pl.ds(key_off, n_k), lanes] <= taken_q, 0.0, NEG_INF)
        bias_t = jnp.where(s_idx >= thr_q, jnp.where(s_idx >= next_q, 0.0, keep_tie), NEG_INF)
        if causal:
            bias_t = jnp.where(key_positions(key_off, n_k) <= query_positions(halves), bias_t, NEG_INF)
        bias = bias_t.T
        s = _dot_nt(qrm_ref[rows, :], kk_ref[pl.ds(key_off, n_k), :])
        s = (s.reshape(nh, B_HEADS, hq, n_k) + bias.reshape(nh, 1, hq, n_k)).reshape(nh * hrows, n_k)
        m_old = m_ref[rows, :]
        m_new = jnp.maximum(m_old, jnp.max(s, axis=1, keepdims=True))
        p = jnp.concatenate(
            [jnp.exp2(s[:, j * LANES:(j + 1) * LANES] - m_new) for j in range(n_k // LANES)],
            axis=1).astype(bf16)
        acc_ref[rows, :] = acc_ref[rows, :] * jnp.exp2(m_old - m_new) + jnp.dot(
            p, vx_ref[pl.ds(key_off, n_k), :], preferred_element_type=f32)
        m_ref[rows, :] = m_new

    def att_body(c, carry):
        att_block(chunk_off(c), ck, (0, 1), False)
        return carry

    lax.fori_loop(0, n_full, att_body, 0)
    att_block(diag, hq, (0,), True)
    att_block(diag, ck, (1,), True)

    for half in range(2):
        qs = slice(half * hq, (half + 1) * hq)
        for p in range(B_HEADS // 2):
            base = half * hrows + 2 * p * hq
            a_even = acc_ref[base:base + hq, :]
            a_odd = acc_ref[base + hq:base + 2 * hq, :]
            pair = jnp.where(lane < HEAD_DIM, a_even / pltpu.roll(a_even, HEAD_DIM, 1),
                             pltpu.roll(a_odd, HEAD_DIM, 1) / a_odd)
            ls = slice(p * LANES, (p + 1) * LANES)
            yb_ref[qs, ls] = (pair * gb_ref[qs, ls].astype(f32)).astype(bf16)


def _sparse_attn(qi, qr, wi, gb, ki, kk, vx):
    b, s, _ = qr.shape
    top_k = min(TOPK_MAX, s // 4)
    blk = lambda i, j: (i, j, 0)
    full = lambda i, j: (i, 0, 0)
    wide = pl.BlockSpec((None, Q_TILE, B_WIDTH), blk)
    keys = pl.BlockSpec((None, s, LANES), full)
    return pl.pallas_call(
        functools.partial(_attn_kernel, top_k=top_k),
        out_shape=jax.ShapeDtypeStruct((b, s, B_WIDTH), jnp.bfloat16),
        grid=(b, s // Q_TILE),
        in_specs=[wide, wide, pl.BlockSpec((None, Q_TILE, LANES), blk), wide, keys, keys, keys],
        out_specs=wide,
        scratch_shapes=[pltpu.VMEM((s, Q_TILE), jnp.float32),
                        pltpu.VMEM((s, Q_TILE), jnp.bfloat16),
                        pltpu.VMEM((s, Q_TILE), jnp.float32),
                        pltpu.VMEM((IDX_HEADS * Q_TILE, LANES), jnp.bfloat16),
                        pltpu.VMEM((B_HEADS * Q_TILE, LANES), jnp.bfloat16),
                        pltpu.VMEM((LANES, Q_TILE), jnp.float32),
                        pltpu.VMEM((QV_ROWS, Q_TILE), jnp.float32),
                        pltpu.VMEM((B_HEADS * Q_TILE, LANES), jnp.float32),
                        pltpu.VMEM((B_HEADS * Q_TILE, LANES), jnp.float32)],
        compiler_params=pltpu.CompilerParams(vmem_limit_bytes=VMEM_LIMIT_BYTES),
        name="sparse_attn",
    )(qi, qr, wi, gb, ki, kk, vx)


def _out_proj_kernel(x_ref, ya_ref, yb_ref, w_ref, mod_ref, g_ref, b_ref, o_ref, *, alpha):
    y = (jnp.dot(ya_ref[...], w_ref[:A_WIDTH, :], preferred_element_type=jnp.float32)
         + jnp.dot(yb_ref[...], w_ref[A_WIDTH:, :], preferred_element_type=jnp.float32))
    r = alpha * x_ref[...] + mod_ref[2] * y
    mu = jnp.mean(r, axis=-1, keepdims=True)
    rc = r - mu
    var = jnp.mean(rc * rc, axis=-1, keepdims=True)
    o_ref[...] = rc * lax.rsqrt(var + LN_EPS) * g_ref[...] + b_ref[...]


def _out_proj(x, ya, yb, w_out, mod4, ln_g, ln_b, alpha, tm):
    b, s, d = x.shape
    row = lambda i, j: (i, j, 0)
    const2 = lambda i, j: (0, 0)
    return pl.pallas_call(
        functools.partial(_out_proj_kernel, alpha=alpha),
        out_shape=jax.ShapeDtypeStruct((b, s, d), jnp.float32),
        grid=(b, s // tm),
        in_specs=[pl.BlockSpec((None, tm, d), row),
                  pl.BlockSpec((None, tm, A_WIDTH), row),
                  pl.BlockSpec((None, tm, B_WIDTH), row),
                  pl.BlockSpec((A_WIDTH + B_WIDTH, d), const2),
                  pl.BlockSpec((None, 3, 1, d), lambda i, j: (i, 0, 0, 0)),
                  pl.BlockSpec((1, d), const2),
                  pl.BlockSpec((1, d), const2)],
        out_specs=pl.BlockSpec((None, tm, d), row),
        compiler_params=pltpu.CompilerParams(vmem_limit_bytes=VMEM_LIMIT_BYTES),
        name="out_proj",
    )(x, ya, yb, w_out, mod4, ln_g, ln_b)


def kernel(x, c, positions, w_ada, b_ada, w_in, v_norm_g, v_norm_b, w_spatial, b_spatial, w_out, ln_g, ln_b):
    depth, d, _ = w_in.shape
    b, s, _ = x.shape
    assert d == D_MODEL and s % Q_TILE == 0
    tm = min(512, s)
    alpha = (2.0 * depth) ** 0.25

    cos, sin = _rope_tables(positions)
    mod = _modulation(c, w_ada, b_ada)
    mod4 = mod.reshape(depth, b, 3, 1, d)

    w_p = _permute_cols(w_in.astype(jnp.bfloat16), _proj_perm())
    w_o = w_out.astype(jnp.bfloat16)
    bs_full = jnp.repeat(jnp.swapaxes(b_spatial, 1, 2), A_GDIM, axis=2)

    for l in range(depth):
        ya, gb, qr, qi, kk, ki, vx, wi = _in_proj(
            x, mod4[l], cos, sin, w_p[l], v_norm_g[l][None, :], v_norm_b[l][None, :],
            w_spatial[l], bs_full[l], tm)
        yb = _sparse_attn(qi, qr, wi, gb, ki, kk, vx)
        x = _out_proj(x, ya, yb, w_o[l], mod4[l], ln_g[l][None, :], ln_b[l][None, :], alpha, tm)
    return x
```

```python
import functools
import math

import numpy as np
import jax
import jax.numpy as jnp
from jax import lax
from jax.experimental import pallas as pl
from jax.experimental.pallas import tpu as pltpu

D_MODEL = 1024
A_WIDTH = D_MODEL // 2
A_GROUPS = 8
A_GDIM = A_WIDTH // A_GROUPS
CHUNK = 128
B_HEADS = 8
HEAD_DIM = 64
HALF = HEAD_DIM // 2
B_WIDTH = B_HEADS * HEAD_DIM
IDX_HEADS = 8
IDX_DIM = HEAD_DIM
TOPK_MAX = 256
ROPE_THETA = 10000.0
LN_EPS = 1e-5

LANES = 128
VMEM_LIMIT_BYTES = 56 * 1024 * 1024

SPLITS = (A_WIDTH, A_WIDTH, A_WIDTH, B_WIDTH, HEAD_DIM, HEAD_DIM, B_WIDTH,
          IDX_HEADS * IDX_DIM, IDX_DIM, IDX_HEADS)
D_IN = sum(SPLITS)
_OFF = np.concatenate([[0], np.cumsum(SPLITS)]).astype(np.int64)
U_OFF, V_OFF, ZA_OFF, Q_OFF, K_OFF, VAL_OFF, ZB_OFF, QI_OFF, KI_OFF, W_OFF = (int(o) for o in _OFF[:-1])

C_U, C_V, C_ZA, C_ZB, C_Q, C_QI = 0, 512, 1024, 1536, 2048, 2560
C_KEYS, C_VW = 3072, 3200
D_PROJ = 3328
W_LANE = HEAD_DIM

Q_TILE = 512
KEY_CHUNK = Q_TILE
COUNT_ROWS = 64
NEG_INF = float("-inf")
M_INIT = -1e30

MIN_NORMAL = float(np.finfo(np.float32).tiny)
KEY32_NEG_INF = -(2 ** 31) + 0x007FFFFF
KEY32_MIN_NORMAL = 0x00800000
KEY32_NEG_MIN_NORMAL = -0x00800000 - 1
KEY16_NEG_INF = -0x8000 + 0x007F
KEY16_MIN_NORMAL = 0x0080
KEY16_NEG_MIN_NORMAL = -0x0080 - 1
KEY16_POS_INF = 0x7F80
BF16_SEARCH_STEPS = 15
F32_SEARCH_MAX_STEPS = 20
F32_STEPS_PER_TEST = 2
COUNT_UNKNOWN = 1e9
QV_PROBE, QV_THR, QV_NEXT, QV_TAKEN, QV_ROWS = 0, 1, 2, 3, 8


def _pair_cols(off):
    cols = []
    for p in range(B_HEADS // 2):
        e, o = off + HEAD_DIM * (2 * p), off + HEAD_DIM * (2 * p + 1)
        cols += [np.arange(e, e + HALF), np.arange(o, o + HALF),
                 np.arange(e + HALF, e + HEAD_DIM), np.arange(o + HALF, o + HEAD_DIM)]
    return np.concatenate(cols)


def _key_cols():
    return np.concatenate([np.arange(K_OFF, K_OFF + HALF), np.arange(KI_OFF, KI_OFF + HALF),
                           np.arange(K_OFF + HALF, K_OFF + HEAD_DIM), np.arange(KI_OFF + HALF, KI_OFF + HEAD_DIM)])


def _proj_perm():
    zero = D_IN
    vw = np.concatenate([np.arange(VAL_OFF, VAL_OFF + HEAD_DIM), np.arange(W_OFF, W_OFF + IDX_HEADS),
                         np.full((LANES - HEAD_DIM - IDX_HEADS,), zero)])
    perm = np.concatenate([
        np.arange(U_OFF, U_OFF + A_WIDTH), np.arange(V_OFF, V_OFF + A_WIDTH),
        np.arange(ZA_OFF, ZA_OFF + A_WIDTH), np.arange(ZB_OFF, ZB_OFF + B_WIDTH),
        _pair_cols(Q_OFF), _pair_cols(QI_OFF), _key_cols(), vw])
    assert perm.shape[0] == D_PROJ
    return perm


def _permute_cols(w, perm):
    runs, start = [], 0
    for j in range(1, len(perm) + 1):
        if j < len(perm):
            a, b = int(perm[j - 1]), int(perm[j])
            if (a == D_IN and b == D_IN) or (a != D_IN and b != D_IN and b == a + 1):
                continue
        runs.append((int(perm[start]), j - start))
        start = j
    parts = []
    for first, n in runs:
        if first == D_IN:
            parts.append(jnp.zeros(w.shape[:-1] + (n,), w.dtype))
        else:
            parts.append(w[..., first:first + n])
    return jnp.concatenate(parts, axis=-1)


def _silu(x):
    return x * (1.0 / (1.0 + jnp.exp(-x)))


def _dot_nt(a, b):
    return lax.dot_general(a, b, (((1,), (1,)), ((), ())), preferred_element_type=jnp.float32)


def _rope_kernel(pos_ref, invf_ref, sgn_ref, cos_ref, sin_ref):
    ang = pos_ref[...].astype(jnp.float32) * invf_ref[...]
    cos_ref[...] = jnp.cos(ang)
    sin_ref[...] = jnp.sin(ang) * sgn_ref[...]


def _rope_tables(positions):
    b, s = positions.shape
    ts = min(s, 1024)
    inv_freq = ROPE_THETA ** (-jnp.arange(0, HEAD_DIM, 2, dtype=jnp.float32) / HEAD_DIM)
    invf = jnp.tile(inv_freq, LANES // HALF)[None, :]
    sgn = jnp.concatenate([-jnp.ones((LANES // 2,), jnp.float32), jnp.ones((LANES // 2,), jnp.float32)])[None, :]
    out = jax.ShapeDtypeStruct((b, s, LANES), jnp.float32)
    return pl.pallas_call(
        _rope_kernel,
        out_shape=(out, out),
        grid=(b, s // ts),
        in_specs=[pl.BlockSpec((None, ts, 1), lambda i, j: (i, j, 0)),
                  pl.BlockSpec((1, LANES), lambda i, j: (0, 0)),
                  pl.BlockSpec((1, LANES), lambda i, j: (0, 0))],
        out_specs=(pl.BlockSpec((None, ts, LANES), lambda i, j: (i, j, 0)),
                   pl.BlockSpec((None, ts, LANES), lambda i, j: (i, j, 0))),
        name="rope_tables",
    )(positions.reshape(b, s, 1), invf, sgn)


def _mod_kernel(c_ref, w_ref, b_ref, o_ref):
    cond = _silu(c_ref[...])
    o_ref[...] = jnp.dot(cond, w_ref[...], preferred_element_type=jnp.float32,
                         precision=lax.Precision.HIGHEST) + b_ref[...]


def _modulation(c, w_ada, b_ada):
    depth, d, d3 = w_ada.shape
    b = c.shape[0]
    return pl.pallas_call(
        _mod_kernel,
        out_shape=jax.ShapeDtypeStruct((depth, b, d3), jnp.float32),
        grid=(depth, d3 // d),
        in_specs=[pl.BlockSpec((b, d), lambda l, j: (0, 0)),
                  pl.BlockSpec((None, d, d), lambda l, j: (l, 0, j)),
                  pl.BlockSpec((None, 1, d), lambda l, j: (l, 0, j))],
        out_specs=pl.BlockSpec((None, b, d), lambda l, j: (l, 0, j)),
        compiler_params=pltpu.CompilerParams(vmem_limit_bytes=VMEM_LIMIT_BYTES),
        name="adaln_mod",
    )(c, w_ada, b_ada.reshape(depth, 1, d3))


def _rope128(x, cos, sin_signed):
    return x * cos + pltpu.roll(x, LANES // 2, 1) * sin_signed


def _in_proj_kernel(x_ref, mod_ref, cos_ref, sin_ref, w_ref, gv_ref, bv_ref, ws_ref, bs_ref,
                    ya_ref, gb_ref, qr_ref, qi_ref, kk_ref, ki_ref, vx_ref, wi_ref):
    tm = x_ref.shape[0]
    x = x_ref[...]
    mu = jnp.mean(x, axis=-1, keepdims=True)
    xc = x - mu
    var = jnp.mean(xc * xc, axis=-1, keepdims=True)
    xn = xc * lax.rsqrt(var + LN_EPS)
    h = xn * (1.0 + mod_ref[1]) + mod_ref[0]
    hb = h.astype(jnp.bfloat16)

    def proj(c0, n):
        return jnp.dot(hb, w_ref[:, c0:c0 + n], preferred_element_type=jnp.float32)

    v = proj(C_V, A_WIDTH)
    vmu = jnp.mean(v, axis=-1, keepdims=True)
    vc = v - vmu
    vvar = jnp.mean(vc * vc, axis=-1, keepdims=True)
    vn = (vc * lax.rsqrt(vvar + LN_EPS) * gv_ref[...] + bv_ref[...]).astype(jnp.bfloat16)
    r_i = lax.broadcasted_iota(jnp.int32, (CHUNK, CHUNK), 0)
    c_i = lax.broadcasted_iota(jnp.int32, (CHUNK, CHUNK), 1)
    wm = [jnp.where(r_i >= c_i, ws_ref[g], 0.0).astype(jnp.bfloat16) for g in range(A_GROUPS)]
    gate_a = _silu(proj(C_ZA, A_WIDTH)) * proj(C_U, A_WIDTH)
    for c in range(tm // CHUNK):
        rows = slice(c * CHUNK, (c + 1) * CHUNK)
        mixed = jnp.concatenate(
            [jnp.dot(wm[g], vn[rows, g * A_GDIM:(g + 1) * A_GDIM], preferred_element_type=jnp.float32)
             for g in range(A_GROUPS)], axis=1) + bs_ref[...]
        ya_ref[rows, :] = (gate_a[rows, :] * mixed).astype(jnp.bfloat16)

    gb_ref[...] = _silu(proj(C_ZB, B_WIDTH)).astype(jnp.bfloat16)
    cos = cos_ref[...]
    sin = sin_ref[...]
    q_scale = HEAD_DIM ** -0.5 * math.log2(math.e)
    qi_scale = IDX_DIM ** -0.5
    q = proj(C_Q, B_WIDTH)
    qi = proj(C_QI, IDX_HEADS * IDX_DIM)
    for p in range(B_WIDTH // LANES):
        ls = slice(p * LANES, (p + 1) * LANES)
        qr_ref[:, ls] = (_rope128(q[:, ls], cos, sin) * q_scale).astype(jnp.bfloat16)
        qi_ref[:, ls] = (_rope128(qi[:, ls], cos, sin) * qi_scale).astype(jnp.bfloat16)
    narrow = proj(C_KEYS, 2 * LANES)
    keys = _rope128(narrow[:, :LANES], cos, sin)
    vw = narrow[:, LANES:]
    lane = lax.broadcasted_iota(jnp.int32, vw.shape, 1)
    odd_quarter = (lane // HALF) % 2 == 1
    kk_ref[...] = jnp.where(odd_quarter, pltpu.roll(keys, HALF, 1), keys).astype(jnp.bfloat16)
    ki_ref[...] = jnp.where(odd_quarter, keys, pltpu.roll(keys, LANES - HALF, 1)).astype(jnp.bfloat16)
    vx_ref[...] = jnp.where(lane < HEAD_DIM, vw, 1.0).astype(jnp.bfloat16)
    wi_ref[...] = vw * (IDX_HEADS ** -0.5)


def _in_proj(x, mod4, cos, sin, w_p, gv, bv, ws, bs_full, tm):
    b, s, d = x.shape
    row = lambda i, j: (i, j, 0)
    const2 = lambda i, j: (0, 0)
    bf = jnp.bfloat16
    shapes = (
        jax.ShapeDtypeStruct((b, s, A_WIDTH), bf),
        jax.ShapeDtypeStruct((b, s, B_WIDTH), bf),
        jax.ShapeDtypeStruct((b, s, B_WIDTH), bf),
        jax.ShapeDtypeStruct((b, s, B_WIDTH), bf),
        jax.ShapeDtypeStruct((b, s, LANES), bf),
        jax.ShapeDtypeStruct((b, s, LANES), bf),
        jax.ShapeDtypeStruct((b, s, LANES), bf),
        jax.ShapeDtypeStruct((b, s, LANES), jnp.float32),
    )
    wide = pl.BlockSpec((None, tm, B_WIDTH), row)
    narrow = pl.BlockSpec((None, tm, LANES), row)
    return pl.pallas_call(
        _in_proj_kernel,
        out_shape=shapes,
        grid=(b, s // tm),
        in_specs=[pl.BlockSpec((None, tm, d), row),
                  pl.BlockSpec((None, 3, 1, d), lambda i, j: (i, 0, 0, 0)),
                  narrow, narrow,
                  pl.BlockSpec((d, D_PROJ), const2),
                  pl.BlockSpec((1, A_WIDTH), const2),
                  pl.BlockSpec((1, A_WIDTH), const2),
                  pl.BlockSpec((A_GROUPS, CHUNK, CHUNK), lambda i, j: (0, 0, 0)),
                  pl.BlockSpec((CHUNK, A_WIDTH), const2)],
        out_specs=(wide, wide, wide, wide, narrow, narrow, narrow, narrow),
        compiler_params=pltpu.CompilerParams(vmem_limit_bytes=VMEM_LIMIT_BYTES),
        name="in_proj",
    )(x, mod4, cos, sin, w_p, gv, bv, ws, bs_full)


def _key32_to_float(key):
    bits = jnp.where(key >= 0, key, key ^ 0x7FFFFFFF)
    return lax.bitcast_convert_type(bits, jnp.float32)


def _key16_to_float(key):
    bits = jnp.where(key >= 0, key, key ^ 0x7FFF)
    return lax.bitcast_convert_type(jnp.left_shift(bits, 16), jnp.float32)


def _key16_to_key32(key):
    return jnp.left_shift(key, 16) + jnp.where(key < 0, 0xFFFF, 0)


def _attn_kernel(qi_ref, qr_ref, wi_ref, gb_ref, ki_ref, kk_ref, vx_ref, yb_ref,
                 score_ref, sb_ref, rank_ref, qim_ref, qrm_ref, wt_ref, qv_ref, m_ref, acc_ref, *, top_k):
    i = pl.program_id(1)
    tq = Q_TILE
    ck = KEY_CHUNK
    hq = tq // 2
    hrows = B_HEADS * hq
    n_full = i
    n_chunks = i + 1
    diag = pl.multiple_of(i * ck, ck)
    f32 = jnp.float32
    bf16 = jnp.bfloat16

    lane = lax.broadcasted_iota(jnp.int32, (hq, LANES), 1)
    even = (lane % HEAD_DIM) < HALF

    def query_positions(halves):
        n_q = len(halves) * hq
        return i * tq + halves[0] * hq + lax.broadcasted_iota(jnp.int32, (1, n_q), 1)

    for half in range(2):
        qs = slice(half * hq, (half + 1) * hq)
        for p in range(B_HEADS // 2):
            ls = slice(p * LANES, (p + 1) * LANES)
            base = half * hrows + 2 * p * hq
            for src, dst in ((qi_ref, qim_ref), (qr_ref, qrm_ref)):
                pair = src[qs, ls]
                zero = jnp.zeros_like(pair)
                dst[base:base + hq, :] = jnp.where(even, pair, zero)
                dst[base + hq:base + 2 * hq, :] = jnp.where(even, zero, pair)
    wt_ref[...] = wi_ref[...].T

    def chunk_off(c):
        return pl.multiple_of(c * ck, ck)

    def half_rows(halves):
        return slice(halves[0] * hrows, (halves[-1] + 1) * hrows)

    def half_lanes(halves):
        return slice(halves[0] * hq, (halves[-1] + 1) * hq)

    def key_positions(key_off, n_k):
        return key_off + lax.broadcasted_iota(jnp.int32, (n_k, 1), 0)

    def idx_block(key_off, n_k, halves, causal):
        logit_t = _dot_nt(ki_ref[pl.ds(key_off, n_k), :], qim_ref[half_rows(halves), :])
        parts = []
        for j, half in enumerate(halves):
            acc = None
            for h in range(IDX_HEADS):
                col = (j * IDX_HEADS + h) * hq
                w_row = wt_ref[W_LANE + h:W_LANE + h + 1, half * hq:(half + 1) * hq]
                term = w_row * jnp.maximum(logit_t[:, col:col + hq], 0.0)
                acc = term if acc is None else acc + term
            parts.append(acc)
        score = parts[0] if len(parts) == 1 else jnp.concatenate(parts, axis=1)
        lanes = half_lanes(halves)
        if causal:
            score = jnp.where(key_positions(key_off, n_k) <= query_positions(halves), score, NEG_INF)
        score_ref[pl.ds(key_off, n_k), lanes] = score
        sb_ref[pl.ds(key_off, n_k), lanes] = score.astype(bf16)

    def idx_body(c, carry):
        idx_block(chunk_off(c), ck, (0, 1), False)
        return carry

    lax.fori_loop(0, n_full, idx_body, 0)
    idx_block(diag, hq, (0,), True)
    idx_block(diag, ck, (1,), True)
    dead_off = pl.multiple_of(diag + hq, hq)
    score_ref[pl.ds(dead_off, ck - hq), 0:hq] = jnp.full((ck - hq, hq), NEG_INF, f32)
    sb_ref[pl.ds(dead_off, ck - hq), 0:hq] = jnp.full((ck - hq, hq), NEG_INF, bf16)

    def count_ge_in(ref, thr, one, zero):
        qv_ref[QV_PROBE:QV_PROBE + 1, :] = thr
        dtype = ref.dtype
        thr_all = jnp.broadcast_to(qv_ref[QV_PROBE:QV_PROBE + 1, :], (COUNT_ROWS, tq)).astype(dtype)
        thr_hi = jnp.broadcast_to(qv_ref[QV_PROBE:QV_PROBE + 1, hq:], (COUNT_ROWS, hq)).astype(dtype)

        def body(c, acc):
            off = chunk_off(c)
            for r in range(0, ck, COUNT_ROWS):
                acc = acc + jnp.where(ref[pl.ds(off + r, COUNT_ROWS), :] >= thr_all, one, zero)
            return acc

        acc = lax.fori_loop(0, n_full, body, jnp.zeros((COUNT_ROWS, tq), dtype))
        for r in range(0, hq, COUNT_ROWS):
            acc = acc + jnp.where(ref[pl.ds(diag + r, COUNT_ROWS), :] >= thr_all, one, zero)
        acc_hi = acc[:, hq:]
        for r in range(hq, ck, COUNT_ROWS):
            acc_hi = acc_hi + jnp.where(ref[pl.ds(diag + r, COUNT_ROWS), hq:] >= thr_hi, one, zero)
        acc = jnp.concatenate([acc[:, :hq], acc_hi], axis=1)
        return jnp.sum(acc.astype(f32), axis=0, keepdims=True)

    def count_ge(thr):
        return count_ge_in(score_ref, thr, 1.0, 0.0)

    def count_ge_bf16(thr):
        return count_ge_in(sb_ref, thr, jnp.ones((), bf16), jnp.zeros((), bf16))

    kf = float(top_k)
    n_nonneg = count_ge_bf16(jnp.zeros((1, tq), f32))
    n_pos = count_ge_bf16(jnp.full((1, tq), MIN_NORMAL, f32))
    is_pos = n_pos >= kf
    is_neg = n_nonneg < kf
    is_zero = jnp.logical_not(jnp.logical_or(is_pos, is_neg))

    lo16 = jnp.where(is_pos, KEY16_MIN_NORMAL, jnp.where(is_neg, KEY16_NEG_INF, 0)).astype(jnp.int32)
    hi16 = jnp.where(is_pos, KEY16_POS_INF + 1, jnp.where(is_neg, KEY16_NEG_MIN_NORMAL + 1, 1)).astype(jnp.int32)

    def bf16_step(_, carry):
        lo, hi = carry
        mid = lo + jnp.right_shift(hi - lo, 1)
        ge = count_ge_bf16(_key16_to_float(mid)) >= kf
        return jnp.where(ge, mid, lo), jnp.where(ge, hi, mid)

    lo16, hi16 = lax.fori_loop(0, BF16_SEARCH_STEPS, bf16_step, (lo16, hi16))

    lo32 = jnp.maximum(_key16_to_key32(lo16) - 0x8001, KEY32_NEG_INF)
    hi32 = _key16_to_key32(lo16 + 1)
    lo32 = jnp.where(is_pos, jnp.maximum(lo32, KEY32_MIN_NORMAL), lo32)
    hi32 = jnp.where(is_neg, jnp.minimum(hi32, KEY32_NEG_MIN_NORMAL + 1), hi32)
    lo32 = jnp.where(is_zero, 0, lo32)
    hi32 = jnp.where(is_zero, 1, hi32)
    n_lo = jnp.where(is_zero, n_nonneg, COUNT_UNKNOWN)

    def f32_active(lo, hi, n_at_lo):
        return jnp.logical_and(n_at_lo != kf, hi - lo > 1)

    def any_true(mask):
        return jnp.max(jnp.where(mask, 1.0, 0.0)) > 0.0

    def f32_cond(carry):
        step, _, _, _, more = carry
        return jnp.logical_and(more, step < F32_SEARCH_MAX_STEPS)

    def f32_halve(lo, hi, n_at_lo):
        active = f32_active(lo, hi, n_at_lo)
        mid = lo + jnp.right_shift(hi - lo, 1)
        n_mid = count_ge(_key32_to_float(mid))
        up = jnp.logical_and(active, n_mid >= kf)
        down = jnp.logical_and(active, n_mid < kf)
        return jnp.where(up, mid, lo), jnp.where(down, mid, hi), jnp.where(up, n_mid, n_at_lo)

    def f32_step(carry):
        step, lo, hi, n_at_lo, _ = carry
        for _ in range(F32_STEPS_PER_TEST):
            lo, hi, n_at_lo = f32_halve(lo, hi, n_at_lo)
        return step + F32_STEPS_PER_TEST, lo, hi, n_at_lo, any_true(f32_active(lo, hi, n_at_lo))

    _, lo32, hi32, n_lo, _ = lax.while_loop(
        f32_cond, f32_step, (jnp.int32(0), lo32, hi32, n_lo, any_true(f32_active(lo32, hi32, n_lo))))

    thr = jnp.where(is_zero, 0.0, _key32_to_float(lo32))
    thr_next = jnp.where(is_zero, MIN_NORMAL, _key32_to_float(lo32 + 1))
    maybe_tie = n_lo > kf

    def rank_ties():
        r_i = lax.broadcasted_iota(jnp.int32, (CHUNK, CHUNK), 0)
        c_i = lax.broadcasted_iota(jnp.int32, (CHUNK, CHUNK), 1)
        tri = jnp.where(c_i <= r_i, 1.0, 0.0).astype(bf16)

        def body(c, carry):
            above, ties_before = carry
            off = chunk_off(c)
            for r in range(0, ck, CHUNK):
                s = score_ref[pl.ds(off + r, CHUNK), :]
                is_above = jnp.where(s >= thr_next, 1.0, 0.0)
                tie = jnp.where(s >= thr, 1.0 - is_above, 0.0).astype(bf16)
                for r2 in range(0, CHUNK, COUNT_ROWS):
                    above = above + is_above[r2:r2 + COUNT_ROWS]
                rank = jnp.dot(tri, tie, preferred_element_type=f32) + ties_before
                rank_ref[pl.ds(off + r, CHUNK), :] = rank
                ties_before = rank[CHUNK - 1:CHUNK, :]
            return above, ties_before

        above, _ = lax.fori_loop(0, n_chunks, body, (jnp.zeros((COUNT_ROWS, tq), f32), jnp.zeros((1, tq), f32)))
        return kf - jnp.sum(above, axis=0, keepdims=True)

    def no_ties():
        def body(c, carry):
            rank_ref[pl.ds(chunk_off(c), ck), :] = jnp.zeros((ck, tq), f32)
            return carry

        lax.fori_loop(0, n_chunks, body, 0)
        return jnp.full((1, tq), COUNT_UNKNOWN, f32)

    ties_taken = lax.cond(any_true(maybe_tie), rank_ties, no_ties)
    qv_ref[QV_THR:QV_THR + 1, :] = thr
    qv_ref[QV_NEXT:QV_NEXT + 1, :] = thr_next
    qv_ref[QV_TAKEN:QV_TAKEN + 1, :] = ties_taken

    m_ref[...] = jnp.full(m_ref.shape, M_INIT, f32)
    acc_ref[...] = jnp.zeros(acc_ref.shape, f32)

    def att_block(key_off, n_k, halves, causal):
        rows, lanes, nh = half_rows(halves), half_lanes(halves), len(halves)
        s_idx = score_ref[pl.ds(key_off, n_k), lanes]
        if nh == 2:
            thr_q, next_q, taken_q = thr, thr_next, ties_taken
        else:
            thr_q = qv_ref[QV_THR:QV_THR + 1, lanes]
            next_q = qv_ref[QV_NEXT:QV_NEXT + 1, lanes]
            taken_q = qv_ref[QV_TAKEN:QV_TAKEN + 1, lanes]
        keep_tie = jnp.where(rank_ref[pl.ds(key_off, n_k), lanes] <= taken_q, 0.0, NEG_INF)
        bias_t = jnp.where(s_idx >= thr_q, jnp.where(s_idx >= next_q, 0.0, keep_tie), NEG_INF)
        if causal:
            bias_t = jnp.where(key_positions(key_off, n_k) <= query_positions(halves), bias_t, NEG_INF)
        bias = bias_t.T
        s = _dot_nt(qrm_ref[rows, :], kk_ref[pl.ds(key_off, n_k), :])
        s = (s.reshape(nh, B_HEADS, hq, n_k) + bias.reshape(nh, 1, hq, n_k)).reshape(nh * hrows, n_k)
        m_old = m_ref[rows, :]
        m_new = jnp.maximum(m_old, jnp.max(s, axis=1, keepdims=True))
        p = jnp.concatenate(
            [jnp.exp2(s[:, j * LANES:(j + 1) * LANES] - m_new) for j in range(n_k // LANES)],
            axis=1).astype(bf16)
        acc_ref[rows, :] = acc_ref[rows, :] * jnp.exp2(m_old - m_new) + jnp.dot(
            p, vx_ref[pl.ds(key_off, n_k), :], preferred_element_type=f32)
        m_ref[rows, :] = m_new

    def att_body(c, carry):
        att_block(chunk_off(c), ck, (0,), False)
        att_block(chunk_off(c), ck, (1,), False)
        return carry

    lax.fori_loop(0, n_full, att_body, 0)
    att_block(diag, hq, (0,), True)
    att_block(diag, ck, (1,), True)

    for half in range(2):
        qs = slice(half * hq, (half + 1) * hq)
        for p in range(B_HEADS // 2):
            base = half * hrows + 2 * p * hq
            a_even = acc_ref[base:base + hq, :]
            a_odd = acc_ref[base + hq:base + 2 * hq, :]
            pair = jnp.where(lane < HEAD_DIM, a_even / pltpu.roll(a_even, HEAD_DIM, 1),
                             pltpu.roll(a_odd, HEAD_DIM, 1) / a_odd)
            ls = slice(p * LANES, (p + 1) * LANES)
            yb_ref[qs, ls] = (pair * gb_ref[qs, ls].astype(f32)).astype(bf16)


def _sparse_attn(qi, qr, wi, gb, ki, kk, vx):
    b, s, _ = qr.shape
    top_k = min(TOPK_MAX, s // 4)
    blk = lambda i, j: (i, j, 0)
    full = lambda i, j: (i, 0, 0)
    wide = pl.BlockSpec((None, Q_TILE, B_WIDTH), blk)
    keys = pl.BlockSpec((None, s, LANES), full)
    return pl.pallas_call(
        functools.partial(_attn_kernel, top_k=top_k),
        out_shape=jax.ShapeDtypeStruct((b, s, B_WIDTH), jnp.bfloat16),
        grid=(b, s // Q_TILE),
        in_specs=[wide, wide, pl.BlockSpec((None, Q_TILE, LANES), blk), wide, keys, keys, keys],
        out_specs=wide,
        scratch_shapes=[pltpu.VMEM((s, Q_TILE), jnp.float32),
                        pltpu.VMEM((s, Q_TILE), jnp.bfloat16),
                        pltpu.VMEM((s, Q_TILE), jnp.float32),
                        pltpu.VMEM((IDX_HEADS * Q_TILE, LANES), jnp.bfloat16),
                        pltpu.VMEM((B_HEADS * Q_TILE, LANES), jnp.bfloat16),
                        pltpu.VMEM((LANES, Q_TILE), jnp.float32),
                        pltpu.VMEM((QV_ROWS, Q_TILE), jnp.float32),
                        pltpu.VMEM((B_HEADS * Q_TILE, LANES), jnp.float32),
                        pltpu.VMEM((B_HEADS * Q_TILE, LANES), jnp.float32)],
        compiler_params=pltpu.CompilerParams(vmem_limit_bytes=VMEM_LIMIT_BYTES),
        name="sparse_attn",
    )(qi, qr, wi, gb, ki, kk, vx)


def _out_proj_kernel(x_ref, ya_ref, yb_ref, w_ref, mod_ref, g_ref, b_ref, o_ref, *, alpha):
    y = (jnp.dot(ya_ref[...], w_ref[:A_WIDTH, :], preferred_element_type=jnp.float32)
         + jnp.dot(yb_ref[...], w_ref[A_WIDTH:, :], preferred_element_type=jnp.float32))
    r = alpha * x_ref[...] + mod_ref[2] * y
    mu = jnp.mean(r, axis=-1, keepdims=True)
    rc = r - mu
    var = jnp.mean(rc * rc, axis=-1, keepdims=True)
    o_ref[...] = rc * lax.rsqrt(var + LN_EPS) * g_ref[...] + b_ref[...]


def _out_proj(x, ya, yb, w_out, mod4, ln_g, ln_b, alpha, tm):
    b, s, d = x.shape
    row = lambda i, j: (i, j, 0)
    const2 = lambda i, j: (0, 0)
    return pl.pallas_call(
        functools.partial(_out_proj_kernel, alpha=alpha),
        out_shape=jax.ShapeDtypeStruct((b, s, d), jnp.float32),
        grid=(b, s // tm),
        in_specs=[pl.BlockSpec((None, tm, d), row),
                  pl.BlockSpec((None, tm, A_WIDTH), row),
                  pl.BlockSpec((None, tm, B_WIDTH), row),
                  pl.BlockSpec((A_WIDTH + B_WIDTH, d), const2),
                  pl.BlockSpec((None, 3, 1, d), lambda i, j: (i, 0, 0, 0)),
                  pl.BlockSpec((1, d), const2),
                  pl.BlockSpec((1, d), const2)],
        out_specs=pl.BlockSpec((None, tm, d), row),
        compiler_params=pltpu.CompilerParams(vmem_limit_bytes=VMEM_LIMIT_BYTES),
        name="out_proj",
    )(x, ya, yb, w_out, mod4, ln_g, ln_b)


def kernel(x, c, positions, w_ada, b_ada, w_in, v_norm_g, v_norm_b, w_spatial, b_spatial, w_out, ln_g, ln_b):
    depth, d, _ = w_in.shape
    b, s, _ = x.shape
    assert d == D_MODEL and s % Q_TILE == 0
    tm = min(512, s)
    alpha = (2.0 * depth) ** 0.25

    cos, sin = _rope_tables(positions)
    mod = _modulation(c, w_ada, b_ada)
    mod4 = mod.reshape(depth, b, 3, 1, d)

    w_p = _permute_cols(w_in.astype(jnp.bfloat16), _proj_perm())
    w_o = w_out.astype(jnp.bfloat16)
    bs_full = jnp.repeat(jnp.swapaxes(b_spatial, 1, 2), A_GDIM, axis=2)

    for l in range(depth):
        ya, gb, qr, qi, kk, ki, vx, wi = _in_proj(
            x, mod4[l], cos, sin, w_p[l], v_norm_g[l][None, :], v_norm_b[l][None, :],
            w_spatial[l], bs_full[l], tm)
        yb = _sparse_attn(qi, qr, wi, gb, ki, kk, vx)
        x = _out_proj(x, ya, yb, w_o[l], mod4[l], ln_g[l][None, :], ln_b[l][None, :], alpha, tm)
    return x
```

```python
import functools
import math

import numpy as np
import jax
import jax.numpy as jnp
from jax import lax
from jax.experimental import pallas as pl
from jax.experimental.pallas import tpu as pltpu

D_MODEL = 1024
A_WIDTH = D_MODEL // 2
A_GROUPS = 8
A_GDIM = A_WIDTH // A_GROUPS
CHUNK = 128
B_HEADS = 8
HEAD_DIM = 64
HALF = HEAD_DIM // 2
B_WIDTH = B_HEADS * HEAD_DIM
IDX_HEADS = 8
IDX_DIM = HEAD_DIM
TOPK_MAX = 256
ROPE_THETA = 10000.0
LN_EPS = 1e-5

LANES = 128
VMEM_LIMIT_BYTES = 56 * 1024 * 1024

SPLITS = (A_WIDTH, A_WIDTH, A_WIDTH, B_WIDTH, HEAD_DIM, HEAD_DIM, B_WIDTH,
          IDX_HEADS * IDX_DIM, IDX_DIM, IDX_HEADS)
D_IN = sum(SPLITS)
_OFF = np.concatenate([[0], np.cumsum(SPLITS)]).astype(np.int64)
U_OFF, V_OFF, ZA_OFF, Q_OFF, K_OFF, VAL_OFF, ZB_OFF, QI_OFF, KI_OFF, W_OFF = (int(o) for o in _OFF[:-1])

C_U, C_V, C_ZA, C_ZB, C_Q, C_QI = 0, 512, 1024, 1536, 2048, 2560
C_KEYS, C_VW = 3072, 3200
D_PROJ = 3328
W_LANE = HEAD_DIM

IN_PROJ_ROWS = 512
OUT_PROJ_ROWS = 1024
Q_TILE = 512
KEY_CHUNK = Q_TILE
Q_PARTS = 4
COUNT_ROWS = 64
NEG_INF = float("-inf")
M_INIT = -1e30

MIN_NORMAL = float(np.finfo(np.float32).tiny)
KEY32_NEG_INF = -(2 ** 31) + 0x007FFFFF
KEY32_MIN_NORMAL = 0x00800000
KEY32_NEG_MIN_NORMAL = -0x00800000 - 1
KEY16_NEG_INF = -0x8000 + 0x007F
KEY16_MIN_NORMAL = 0x0080
KEY16_NEG_MIN_NORMAL = -0x0080 - 1
KEY16_POS_INF = 0x7F80
BF16_SEARCH_STEPS = 15
F32_SEARCH_MAX_STEPS = 20
F32_STEPS_PER_TEST = 2
COUNT_UNKNOWN = 1e9
QV_PROBE, QV_THR, QV_NEXT, QV_TAKEN, QV_ROWS = 0, 1, 2, 3, 8


def _pair_cols(off):
    cols = []
    for p in range(B_HEADS // 2):
        e, o = off + HEAD_DIM * (2 * p), off + HEAD_DIM * (2 * p + 1)
        cols += [np.arange(e, e + HALF), np.arange(o, o + HALF),
                 np.arange(e + HALF, e + HEAD_DIM), np.arange(o + HALF, o + HEAD_DIM)]
    return np.concatenate(cols)


def _key_cols():
    return np.concatenate([np.arange(K_OFF, K_OFF + HALF), np.arange(KI_OFF, KI_OFF + HALF),
                           np.arange(K_OFF + HALF, K_OFF + HEAD_DIM), np.arange(KI_OFF + HALF, KI_OFF + HEAD_DIM)])


def _proj_perm():
    zero = D_IN
    vw = np.concatenate([np.arange(VAL_OFF, VAL_OFF + HEAD_DIM), np.arange(W_OFF, W_OFF + IDX_HEADS),
                         np.full((LANES - HEAD_DIM - IDX_HEADS,), zero)])
    perm = np.concatenate([
        np.arange(U_OFF, U_OFF + A_WIDTH), np.arange(V_OFF, V_OFF + A_WIDTH),
        np.arange(ZA_OFF, ZA_OFF + A_WIDTH), np.arange(ZB_OFF, ZB_OFF + B_WIDTH),
        _pair_cols(Q_OFF), _pair_cols(QI_OFF), _key_cols(), vw])
    assert perm.shape[0] == D_PROJ
    return perm


def _permute_cols(w, perm):
    runs, start = [], 0
    for j in range(1, len(perm) + 1):
        if j < len(perm):
            a, b = int(perm[j - 1]), int(perm[j])
            if (a == D_IN and b == D_IN) or (a != D_IN and b != D_IN and b == a + 1):
                continue
        runs.append((int(perm[start]), j - start))
        start = j
    parts = []
    for first, n in runs:
        if first == D_IN:
            parts.append(jnp.zeros(w.shape[:-1] + (n,), w.dtype))
        else:
            parts.append(w[..., first:first + n])
    return jnp.concatenate(parts, axis=-1)


def _silu(x):
    return x * (1.0 / (1.0 + jnp.exp(-x)))


def _dot_nt(a, b):
    return lax.dot_general(a, b, (((1,), (1,)), ((), ())), preferred_element_type=jnp.float32)


def _rope_kernel(pos_ref, invf_ref, sgn_ref, cos_ref, sin_ref):
    ang = pos_ref[...].astype(jnp.float32) * invf_ref[...]
    cos_ref[...] = jnp.cos(ang)
    sin_ref[...] = jnp.sin(ang) * sgn_ref[...]


def _rope_tables(positions):
    b, s = positions.shape
    ts = min(s, 1024)
    inv_freq = ROPE_THETA ** (-jnp.arange(0, HEAD_DIM, 2, dtype=jnp.float32) / HEAD_DIM)
    invf = jnp.tile(inv_freq, LANES // HALF)[None, :]
    sgn = jnp.concatenate([-jnp.ones((LANES // 2,), jnp.float32), jnp.ones((LANES // 2,), jnp.float32)])[None, :]
    out = jax.ShapeDtypeStruct((b, s, LANES), jnp.float32)
    return pl.pallas_call(
        _rope_kernel,
        out_shape=(out, out),
        grid=(b, s // ts),
        in_specs=[pl.BlockSpec((None, ts, 1), lambda i, j: (i, j, 0)),
                  pl.BlockSpec((1, LANES), lambda i, j: (0, 0)),
                  pl.BlockSpec((1, LANES), lambda i, j: (0, 0))],
        out_specs=(pl.BlockSpec((None, ts, LANES), lambda i, j: (i, j, 0)),
                   pl.BlockSpec((None, ts, LANES), lambda i, j: (i, j, 0))),
        name="rope_tables",
    )(positions.reshape(b, s, 1), invf, sgn)


def _mod_kernel(c_ref, w_ref, b_ref, o_ref):
    cond = _silu(c_ref[...])
    o_ref[...] = jnp.dot(cond, w_ref[...], preferred_element_type=jnp.float32,
                         precision=lax.Precision.HIGHEST) + b_ref[...]


def _modulation(c, w_ada, b_ada):
    depth, d, d3 = w_ada.shape
    b = c.shape[0]
    return pl.pallas_call(
        _mod_kernel,
        out_shape=jax.ShapeDtypeStruct((depth, b, d3), jnp.float32),
        grid=(depth, d3 // d),
        in_specs=[pl.BlockSpec((b, d), lambda l, j: (0, 0)),
                  pl.BlockSpec((None, d, d), lambda l, j: (l, 0, j)),
                  pl.BlockSpec((None, 1, d), lambda l, j: (l, 0, j))],
        out_specs=pl.BlockSpec((None, b, d), lambda l, j: (l, 0, j)),
        compiler_params=pltpu.CompilerParams(vmem_limit_bytes=VMEM_LIMIT_BYTES),
        name="adaln_mod",
    )(c, w_ada, b_ada.reshape(depth, 1, d3))


def _rope128(x, cos, sin_signed):
    return x * cos + pltpu.roll(x, LANES // 2, 1) * sin_signed


def _in_proj_kernel(x_ref, mod_ref, cos_ref, sin_ref, w_ref, gv_ref, bv_ref, ws_ref, bs_ref,
                    ya_ref, gb_ref, qr_ref, qi_ref, kk_ref, ki_ref, vx_ref, wi_ref):
    tm = x_ref.shape[0]
    x = x_ref[...]
    mu = jnp.mean(x, axis=-1, keepdims=True)
    xc = x - mu
    var = jnp.mean(xc * xc, axis=-1, keepdims=True)
    xn = xc * lax.rsqrt(var + LN_EPS)
    h = xn * (1.0 + mod_ref[1]) + mod_ref[0]
    hb = h.astype(jnp.bfloat16)

    def proj(c0, n):
        return jnp.dot(hb, w_ref[:, c0:c0 + n], preferred_element_type=jnp.float32)

    v = proj(C_V, A_WIDTH)
    vmu = jnp.mean(v, axis=-1, keepdims=True)
    vc = v - vmu
    vvar = jnp.mean(vc * vc, axis=-1, keepdims=True)
    vn = (vc * lax.rsqrt(vvar + LN_EPS) * gv_ref[...] + bv_ref[...]).astype(jnp.bfloat16)
    r_i = lax.broadcasted_iota(jnp.int32, (CHUNK, CHUNK), 0)
    c_i = lax.broadcasted_iota(jnp.int32, (CHUNK, CHUNK), 1)
    wm = [jnp.where(r_i >= c_i, ws_ref[g], 0.0).astype(jnp.bfloat16) for g in range(A_GROUPS)]
    gate_a = _silu(proj(C_ZA, A_WIDTH)) * proj(C_U, A_WIDTH)
    for c in range(tm // CHUNK):
        rows = slice(c * CHUNK, (c + 1) * CHUNK)
        mixed = jnp.concatenate(
            [jnp.dot(wm[g], vn[rows, g * A_GDIM:(g + 1) * A_GDIM], preferred_element_type=jnp.float32)
             for g in range(A_GROUPS)], axis=1) + bs_ref[...]
        ya_ref[rows, :] = (gate_a[rows, :] * mixed).astype(jnp.bfloat16)

    gb_ref[...] = _silu(proj(C_ZB, B_WIDTH)).astype(jnp.bfloat16)
    cos = cos_ref[...]
    sin = sin_ref[...]
    q_scale = HEAD_DIM ** -0.5 * math.log2(math.e)
    qi_scale = IDX_DIM ** -0.5
    q = proj(C_Q, B_WIDTH)
    qi = proj(C_QI, IDX_HEADS * IDX_DIM)
    for p in range(B_WIDTH // LANES):
        ls = slice(p * LANES, (p + 1) * LANES)
        qr_ref[:, ls] = (_rope128(q[:, ls], cos, sin) * q_scale).astype(jnp.bfloat16)
        qi_ref[:, ls] = (_rope128(qi[:, ls], cos, sin) * qi_scale).astype(jnp.bfloat16)
    narrow = proj(C_KEYS, 2 * LANES)
    keys = _rope128(narrow[:, :LANES], cos, sin)
    vw = narrow[:, LANES:]
    lane = lax.broadcasted_iota(jnp.int32, vw.shape, 1)
    odd_quarter = (lane // HALF) % 2 == 1
    kk_ref[...] = jnp.where(odd_quarter, pltpu.roll(keys, HALF, 1), keys).astype(jnp.bfloat16)
    ki_ref[...] = jnp.where(odd_quarter, keys, pltpu.roll(keys, LANES - HALF, 1)).astype(jnp.bfloat16)
    vx_ref[...] = jnp.where(lane < HEAD_DIM, vw, 1.0).astype(jnp.bfloat16)
    wi_ref[...] = vw * (IDX_HEADS ** -0.5)


def _in_proj(x, mod4, cos, sin, w_p, gv, bv, ws, bs_full, tm):
    b, s, d = x.shape
    row = lambda i, j: (i, j, 0)
    const2 = lambda i, j: (0, 0)
    bf = jnp.bfloat16
    shapes = (
        jax.ShapeDtypeStruct((b, s, A_WIDTH), bf),
        jax.ShapeDtypeStruct((b, s, B_WIDTH), bf),
        jax.ShapeDtypeStruct((b, s, B_WIDTH), bf),
        jax.ShapeDtypeStruct((b, s, B_WIDTH), bf),
        jax.ShapeDtypeStruct((b, s, LANES), bf),
        jax.ShapeDtypeStruct((b, s, LANES), bf),
        jax.ShapeDtypeStruct((b, s, LANES), bf),
        jax.ShapeDtypeStruct((b, s, LANES), jnp.float32),
    )
    wide = pl.BlockSpec((None, tm, B_WIDTH), row)
    narrow = pl.BlockSpec((None, tm, LANES), row)
    return pl.pallas_call(
        _in_proj_kernel,
        out_shape=shapes,
        grid=(b, s // tm),
        in_specs=[pl.BlockSpec((None, tm, d), row),
                  pl.BlockSpec((None, 3, 1, d), lambda i, j: (i, 0, 0, 0)),
                  narrow, narrow,
                  pl.BlockSpec((d, D_PROJ), const2),
                  pl.BlockSpec((1, A_WIDTH), const2),
                  pl.BlockSpec((1, A_WIDTH), const2),
                  pl.BlockSpec((A_GROUPS, CHUNK, CHUNK), lambda i, j: (0, 0, 0)),
                  pl.BlockSpec((CHUNK, A_WIDTH), const2)],
        out_specs=(wide, wide, wide, wide, narrow, narrow, narrow, narrow),
        compiler_params=pltpu.CompilerParams(vmem_limit_bytes=VMEM_LIMIT_BYTES),
        name="in_proj",
    )(x, mod4, cos, sin, w_p, gv, bv, ws, bs_full)


def _key32_to_float(key):
    bits = jnp.where(key >= 0, key, key ^ 0x7FFFFFFF)
    return lax.bitcast_convert_type(bits, jnp.float32)


def _key16_to_float(key):
    bits = jnp.where(key >= 0, key, key ^ 0x7FFF)
    return lax.bitcast_convert_type(jnp.left_shift(bits, 16), jnp.float32)


def _key16_to_key32(key):
    return jnp.left_shift(key, 16) + jnp.where(key < 0, 0xFFFF, 0)


def _attn_kernel(qi_ref, qr_ref, wi_ref, gb_ref, ki_ref, kk_ref, vx_ref, yb_ref,
                 score_ref, sb_ref, rank_ref, qim_ref, qrm_ref, wt_ref, qv_ref, m_ref, acc_ref, *, top_k):
    i = pl.program_id(1)
    tq = Q_TILE
    ck = KEY_CHUNK
    hq = tq // Q_PARTS
    hrows = B_HEADS * hq
    all_parts = tuple(range(Q_PARTS))
    n_full = i
    n_chunks = i + 1
    diag = pl.multiple_of(i * ck, ck)
    f32 = jnp.float32
    bf16 = jnp.bfloat16

    lane = lax.broadcasted_iota(jnp.int32, (hq, LANES), 1)
    even = (lane % HEAD_DIM) < HALF

    def query_positions(halves):
        n_q = len(halves) * hq
        return i * tq + halves[0] * hq + lax.broadcasted_iota(jnp.int32, (1, n_q), 1)

    for half in all_parts:
        qs = slice(half * hq, (half + 1) * hq)
        for p in range(B_HEADS // 2):
            ls = slice(p * LANES, (p + 1) * LANES)
            base = half * hrows + 2 * p * hq
            for src, dst in ((qi_ref, qim_ref), (qr_ref, qrm_ref)):
                pair = src[qs, ls]
                zero = jnp.zeros_like(pair)
                dst[base:base + hq, :] = jnp.where(even, pair, zero)
                dst[base + hq:base + 2 * hq, :] = jnp.where(even, zero, pair)
    wt_ref[...] = wi_ref[...].T

    def chunk_off(c):
        return pl.multiple_of(c * ck, ck)

    def half_rows(halves):
        return slice(halves[0] * hrows, (halves[-1] + 1) * hrows)

    def half_lanes(halves):
        return slice(halves[0] * hq, (halves[-1] + 1) * hq)

    def key_positions(key_off, n_k):
        return key_off + lax.broadcasted_iota(jnp.int32, (n_k, 1), 0)

    def idx_block(key_off, n_k, halves, causal):
        logit_t = _dot_nt(ki_ref[pl.ds(key_off, n_k), :], qim_ref[half_rows(halves), :])
        parts = []
        for j, half in enumerate(halves):
            acc = None
            for h in range(IDX_HEADS):
                col = (j * IDX_HEADS + h) * hq
                w_row = wt_ref[W_LANE + h:W_LANE + h + 1, half * hq:(half + 1) * hq]
                term = w_row * jnp.maximum(logit_t[:, col:col + hq], 0.0)
                acc = term if acc is None else acc + term
            parts.append(acc)
        score = parts[0] if len(parts) == 1 else jnp.concatenate(parts, axis=1)
        lanes = half_lanes(halves)
        if causal:
            score = jnp.where(key_positions(key_off, n_k) <= query_positions(halves), score, NEG_INF)
        score_ref[pl.ds(key_off, n_k), lanes] = score
        sb_ref[pl.ds(key_off, n_k), lanes] = score.astype(bf16)

    def idx_body(c, carry):
        idx_block(chunk_off(c), ck, all_parts, False)
        return carry

    lax.fori_loop(0, n_full, idx_body, 0)
    for part in all_parts:
        n_vis = (part + 1) * hq
        idx_block(diag, n_vis, (part,), True)
        if n_vis < ck:
            dead_off = pl.multiple_of(diag + n_vis, hq)
            lanes = slice(part * hq, (part + 1) * hq)
            score_ref[pl.ds(dead_off, ck - n_vis), lanes] = jnp.full((ck - n_vis, hq), NEG_INF, f32)
            sb_ref[pl.ds(dead_off, ck - n_vis), lanes] = jnp.full((ck - n_vis, hq), NEG_INF, bf16)

    def count_ge_in(ref, thr, one, zero):
        qv_ref[QV_PROBE:QV_PROBE + 1, :] = thr
        dtype = ref.dtype

        def thr_from(part):
            row = qv_ref[QV_PROBE:QV_PROBE + 1, part * hq:]
            return jnp.broadcast_to(row, (COUNT_ROWS, tq - part * hq)).astype(dtype)

        thr_all = thr_from(0)

        def body(c, acc):
            off = chunk_off(c)
            for r in range(0, ck, COUNT_ROWS):
                acc = acc + jnp.where(ref[pl.ds(off + r, COUNT_ROWS), :] >= thr_all, one, zero)
            return acc

        acc = lax.fori_loop(0, n_full, body, jnp.zeros((COUNT_ROWS, tq), dtype))
        done = []
        for g in all_parts:
            thr_g = thr_all if g == 0 else thr_from(g)
            for r in range(g * hq, (g + 1) * hq, COUNT_ROWS):
                acc = acc + jnp.where(ref[pl.ds(diag + r, COUNT_ROWS), g * hq:] >= thr_g, one, zero)
            if g < Q_PARTS - 1:
                done.append(acc[:, :hq])
                acc = acc[:, hq:]
            else:
                done.append(acc)
        acc = jnp.concatenate(done, axis=1)
        return jnp.sum(acc.astype(f32), axis=0, keepdims=True)

    def count_ge(thr):
        return count_ge_in(score_ref, thr, 1.0, 0.0)

    def count_ge_bf16(thr):
        return count_ge_in(sb_ref, thr, jnp.ones((), bf16), jnp.zeros((), bf16))

    kf = float(top_k)
    n_nonneg = count_ge_bf16(jnp.zeros((1, tq), f32))
    n_pos = count_ge_bf16(jnp.full((1, tq), MIN_NORMAL, f32))
    is_pos = n_pos >= kf
    is_neg = n_nonneg < kf
    is_zero = jnp.logical_not(jnp.logical_or(is_pos, is_neg))

    lo16 = jnp.where(is_pos, KEY16_MIN_NORMAL, jnp.where(is_neg, KEY16_NEG_INF, 0)).astype(jnp.int32)
    hi16 = jnp.where(is_pos, KEY16_POS_INF + 1, jnp.where(is_neg, KEY16_NEG_MIN_NORMAL + 1, 1)).astype(jnp.int32)

    def bf16_step(_, carry):
        lo, hi = carry
        mid = lo + jnp.right_shift(hi - lo, 1)
        ge = count_ge_bf16(_key16_to_float(mid)) >= kf
        return jnp.where(ge, mid, lo), jnp.where(ge, hi, mid)

    lo16, hi16 = lax.fori_loop(0, BF16_SEARCH_STEPS, bf16_step, (lo16, hi16))

    lo32 = jnp.maximum(_key16_to_key32(lo16) - 0x8001, KEY32_NEG_INF)
    hi32 = _key16_to_key32(lo16 + 1)
    lo32 = jnp.where(is_pos, jnp.maximum(lo32, KEY32_MIN_NORMAL), lo32)
    hi32 = jnp.where(is_neg, jnp.minimum(hi32, KEY32_NEG_MIN_NORMAL + 1), hi32)
    lo32 = jnp.where(is_zero, 0, lo32)
    hi32 = jnp.where(is_zero, 1, hi32)
    n_lo = jnp.where(is_zero, n_nonneg, COUNT_UNKNOWN)

    def f32_active(lo, hi, n_at_lo):
        return jnp.logical_and(n_at_lo != kf, hi - lo > 1)

    def any_true(mask):
        return jnp.max(jnp.where(mask, 1.0, 0.0)) > 0.0

    def f32_cond(carry):
        step, _, _, _, more = carry
        return jnp.logical_and(more, step < F32_SEARCH_MAX_STEPS)

    def f32_halve(lo, hi, n_at_lo):
        active = f32_active(lo, hi, n_at_lo)
        mid = lo + jnp.right_shift(hi - lo, 1)
        n_mid = count_ge(_key32_to_float(mid))
        up = jnp.logical_and(active, n_mid >= kf)
        down = jnp.logical_and(active, n_mid < kf)
        return jnp.where(up, mid, lo), jnp.where(down, mid, hi), jnp.where(up, n_mid, n_at_lo)

    def f32_step(carry):
        step, lo, hi, n_at_lo, _ = carry
        for _ in range(F32_STEPS_PER_TEST):
            lo, hi, n_at_lo = f32_halve(lo, hi, n_at_lo)
        return step + F32_STEPS_PER_TEST, lo, hi, n_at_lo, any_true(f32_active(lo, hi, n_at_lo))

    _, lo32, hi32, n_lo, _ = lax.while_loop(
        f32_cond, f32_step, (jnp.int32(0), lo32, hi32, n_lo, any_true(f32_active(lo32, hi32, n_lo))))

    thr = jnp.where(is_zero, 0.0, _key32_to_float(lo32))
    thr_next = jnp.where(is_zero, MIN_NORMAL, _key32_to_float(lo32 + 1))
    maybe_tie = n_lo > kf

    def rank_ties():
        r_i = lax.broadcasted_iota(jnp.int32, (CHUNK, CHUNK), 0)
        c_i = lax.broadcasted_iota(jnp.int32, (CHUNK, CHUNK), 1)
        tri = jnp.where(c_i <= r_i, 1.0, 0.0).astype(bf16)

        def body(c, carry):
            above, ties_before = carry
            off = chunk_off(c)
            for r in range(0, ck, CHUNK):
                s = score_ref[pl.ds(off + r, CHUNK), :]
                is_above = jnp.where(s >= thr_next, 1.0, 0.0)
                tie = jnp.where(s >= thr, 1.0 - is_above, 0.0).astype(bf16)
                for r2 in range(0, CHUNK, COUNT_ROWS):
                    above = above + is_above[r2:r2 + COUNT_ROWS]
                rank = jnp.dot(tri, tie, preferred_element_type=f32) + ties_before
                rank_ref[pl.ds(off + r, CHUNK), :] = rank
                ties_before = rank[CHUNK - 1:CHUNK, :]
            return above, ties_before

        above, _ = lax.fori_loop(0, n_chunks, body, (jnp.zeros((COUNT_ROWS, tq), f32), jnp.zeros((1, tq), f32)))
        return kf - jnp.sum(above, axis=0, keepdims=True)

    def no_ties():
        def body(c, carry):
            rank_ref[pl.ds(chunk_off(c), ck), :] = jnp.zeros((ck, tq), f32)
            return carry

        lax.fori_loop(0, n_chunks, body, 0)
        return jnp.full((1, tq), COUNT_UNKNOWN, f32)

    ties_taken = lax.cond(any_true(maybe_tie), rank_ties, no_ties)
    qv_ref[QV_THR:QV_THR + 1, :] = thr
    qv_ref[QV_NEXT:QV_NEXT + 1, :] = thr_next
    qv_ref[QV_TAKEN:QV_TAKEN + 1, :] = ties_taken

    m_ref[...] = jnp.full(m_ref.shape, M_INIT, f32)
    acc_ref[...] = jnp.zeros(acc_ref.shape, f32)

    def att_block(key_off, n_k, halves, causal):
        rows, lanes, nh = half_rows(halves), half_lanes(halves), len(halves)
        s_idx = score_ref[pl.ds(key_off, n_k), lanes]
        thr_q = qv_ref[QV_THR:QV_THR + 1, lanes]
        next_q = qv_ref[QV_NEXT:QV_NEXT + 1, lanes]
        taken_q = qv_ref[QV_TAKEN:QV_TAKEN + 1, lanes]
        keep_tie = jnp.where(rank_ref[pl.ds(key_off, n_k), lanes] <= taken_q, 0.0, NEG_INF)
        bias_t = jnp.where(s_idx >= thr_q, jnp.where(s_idx >= next_q, 0.0, keep_tie), NEG_INF)
        if causal:
            bias_t = jnp.where(key_positions(key_off, n_k) <= query_positions(halves), bias_t, NEG_INF)
        bias = bias_t.T
        s = _dot_nt(qrm_ref[rows, :], kk_ref[pl.ds(key_off, n_k), :])
        s = (s.reshape(nh, B_HEADS, hq, n_k) + bias.reshape(nh, 1, hq, n_k)).reshape(nh * hrows, n_k)
        m_old = m_ref[rows, :]
        m_new = jnp.maximum(m_old, jnp.max(s, axis=1, keepdims=True))
        p = jnp.concatenate(
            [jnp.exp2(s[:, j * LANES:(j + 1) * LANES] - m_new) for j in range(n_k // LANES)],
            axis=1).astype(bf16)
        acc_ref[rows, :] = acc_ref[rows, :] * jnp.exp2(m_old - m_new) + jnp.dot(
            p, vx_ref[pl.ds(key_off, n_k), :], preferred_element_type=f32)
        m_ref[rows, :] = m_new

    def att_body(c, carry):
        for part in all_parts:
            att_block(chunk_off(c), ck, (part,), False)
        return carry

    lax.fori_loop(0, n_full, att_body, 0)
    for part in all_parts:
        att_block(diag, (part + 1) * hq, (part,), True)

    for half in all_parts:
        qs = slice(half * hq, (half + 1) * hq)
        for p in range(B_HEADS // 2):
            base = half * hrows + 2 * p * hq
            a_even = acc_ref[base:base + hq, :]
            a_odd = acc_ref[base + hq:base + 2 * hq, :]
            pair = jnp.where(lane < HEAD_DIM, a_even / pltpu.roll(a_even, HEAD_DIM, 1),
                             pltpu.roll(a_odd, HEAD_DIM, 1) / a_odd)
            ls = slice(p * LANES, (p + 1) * LANES)
            yb_ref[qs, ls] = (pair * gb_ref[qs, ls].astype(f32)).astype(bf16)


def _sparse_attn(qi, qr, wi, gb, ki, kk, vx):
    b, s, _ = qr.shape
    top_k = min(TOPK_MAX, s // 4)
    blk = lambda i, j: (i, j, 0)
    full = lambda i, j: (i, 0, 0)
    wide = pl.BlockSpec((None, Q_TILE, B_WIDTH), blk)
    keys = pl.BlockSpec((None, s, LANES), full)
    return pl.pallas_call(
        functools.partial(_attn_kernel, top_k=top_k),
        out_shape=jax.ShapeDtypeStruct((b, s, B_WIDTH), jnp.bfloat16),
        grid=(b, s // Q_TILE),
        in_specs=[wide, wide, pl.BlockSpec((None, Q_TILE, LANES), blk), wide, keys, keys, keys],
        out_specs=wide,
        scratch_shapes=[pltpu.VMEM((s, Q_TILE), jnp.float32),
                        pltpu.VMEM((s, Q_TILE), jnp.bfloat16),
                        pltpu.VMEM((s, Q_TILE), jnp.float32),
                        pltpu.VMEM((IDX_HEADS * Q_TILE, LANES), jnp.bfloat16),
                        pltpu.VMEM((B_HEADS * Q_TILE, LANES), jnp.bfloat16),
                        pltpu.VMEM((LANES, Q_TILE), jnp.float32),
                        pltpu.VMEM((QV_ROWS, Q_TILE), jnp.float32),
                        pltpu.VMEM((B_HEADS * Q_TILE, LANES), jnp.float32),
                        pltpu.VMEM((B_HEADS * Q_TILE, LANES), jnp.float32)],
        compiler_params=pltpu.CompilerParams(vmem_limit_bytes=VMEM_LIMIT_BYTES),
        name="sparse_attn",
    )(qi, qr, wi, gb, ki, kk, vx)


def _out_proj_kernel(x_ref, ya_ref, yb_ref, w_ref, mod_ref, g_ref, b_ref, o_ref, *, alpha):
    y = (jnp.dot(ya_ref[...], w_ref[:A_WIDTH, :], preferred_element_type=jnp.float32)
         + jnp.dot(yb_ref[...], w_ref[A_WIDTH:, :], preferred_element_type=jnp.float32))
    r = alpha * x_ref[...] + mod_ref[2] * y
    mu = jnp.mean(r, axis=-1, keepdims=True)
    rc = r - mu
    var = jnp.mean(rc * rc, axis=-1, keepdims=True)
    o_ref[...] = rc * lax.rsqrt(var + LN_EPS) * g_ref[...] + b_ref[...]


def _out_proj(x, ya, yb, w_out, mod4, ln_g, ln_b, alpha, tm):
    b, s, d = x.shape
    row = lambda i, j: (i, j, 0)
    const2 = lambda i, j: (0, 0)
    return pl.pallas_call(
        functools.partial(_out_proj_kernel, alpha=alpha),
        out_shape=jax.ShapeDtypeStruct((b, s, d), jnp.float32),
        grid=(b, s // tm),
        in_specs=[pl.BlockSpec((None, tm, d), row),
                  pl.BlockSpec((None, tm, A_WIDTH), row),
                  pl.BlockSpec((None, tm, B_WIDTH), row),
                  pl.BlockSpec((A_WIDTH + B_WIDTH, d), const2),
                  pl.BlockSpec((None, 3, 1, d), lambda i, j: (i, 0, 0, 0)),
                  pl.BlockSpec((1, d), const2),
                  pl.BlockSpec((1, d), const2)],
        out_specs=pl.BlockSpec((None, tm, d), row),
        compiler_params=pltpu.CompilerParams(vmem_limit_bytes=VMEM_LIMIT_BYTES),
        name="out_proj",
    )(x, ya, yb, w_out, mod4, ln_g, ln_b)


def kernel(x, c, positions, w_ada, b_ada, w_in, v_norm_g, v_norm_b, w_spatial, b_spatial, w_out, ln_g, ln_b):
    depth, d, _ = w_in.shape
    b, s, _ = x.shape
    assert d == D_MODEL and s % Q_TILE == 0
    tm = min(IN_PROJ_ROWS, s)
    tm_out = min(OUT_PROJ_ROWS, s)
    alpha = (2.0 * depth) ** 0.25

    cos, sin = _rope_tables(positions)
    mod = _modulation(c, w_ada, b_ada)
    mod4 = mod.reshape(depth, b, 3, 1, d)

    w_p = _permute_cols(w_in.astype(jnp.bfloat16), _proj_perm())
    w_o = w_out.astype(jnp.bfloat16)
    bs_full = jnp.repeat(jnp.swapaxes(b_spatial, 1, 2), A_GDIM, axis=2)

    for l in range(depth):
        ya, gb, qr, qi, kk, ki, vx, wi = _in_proj(
            x, mod4[l], cos, sin, w_p[l], v_norm_g[l][None, :], v_norm_b[l][None, :],
            w_spatial[l], bs_full[l], tm)
        yb = _sparse_attn(qi, qr, wi, gb, ki, kk, vx)
        x = _out_proj(x, ya, yb, w_o[l], mod4[l], ln_g[l][None, :], ln_b[l][None, :], alpha, tm_out)
    return x
```

```python
import functools
import math

import numpy as np
import jax
import jax.numpy as jnp
from jax import lax
from jax.experimental import pallas as pl
from jax.experimental.pallas import tpu as pltpu

D_MODEL = 1024
A_WIDTH = D_MODEL // 2
A_GROUPS = 8
A_GDIM = A_WIDTH // A_GROUPS
CHUNK = 128
B_HEADS = 8
HEAD_DIM = 64
HALF = HEAD_DIM // 2
B_WIDTH = B_HEADS * HEAD_DIM
IDX_HEADS = 8
IDX_DIM = HEAD_DIM
TOPK_MAX = 256
ROPE_THETA = 10000.0
LN_EPS = 1e-5

LANES = 128
VMEM_LIMIT_BYTES = 56 * 1024 * 1024

SPLITS = (A_WIDTH, A_WIDTH, A_WIDTH, B_WIDTH, HEAD_DIM, HEAD_DIM, B_WIDTH,
          IDX_HEADS * IDX_DIM, IDX_DIM, IDX_HEADS)
D_IN = sum(SPLITS)
_OFF = np.concatenate([[0], np.cumsum(SPLITS)]).astype(np.int64)
U_OFF, V_OFF, ZA_OFF, Q_OFF, K_OFF, VAL_OFF, ZB_OFF, QI_OFF, KI_OFF, W_OFF = (int(o) for o in _OFF[:-1])

C_U, C_V, C_ZA, C_ZB, C_Q, C_QI = 0, 512, 1024, 1536, 2048, 2560
C_KEYS, C_VW = 3072, 3200
D_PROJ = 3328
W_LANE = HEAD_DIM

IN_PROJ_ROWS = 1024
OUT_PROJ_ROWS = 2048
Q_TILE = 512
KEY_CHUNK = Q_TILE
Q_PARTS = 4
COUNT_ROWS = 64
NEG_INF = float("-inf")
M_INIT = -1e30

MIN_NORMAL = float(np.finfo(np.float32).tiny)
KEY32_NEG_INF = -(2 ** 31) + 0x007FFFFF
KEY32_MIN_NORMAL = 0x00800000
KEY32_NEG_MIN_NORMAL = -0x00800000 - 1
KEY16_NEG_INF = -0x8000 + 0x007F
KEY16_MIN_NORMAL = 0x0080
KEY16_NEG_MIN_NORMAL = -0x0080 - 1
KEY16_POS_INF = 0x7F80
BF16_SEARCH_STEPS = 15
F32_SEARCH_MAX_STEPS = 20
F32_STEPS_PER_TEST = 2
COUNT_UNKNOWN = 1e9
QV_PROBE, QV_THR, QV_NEXT, QV_TAKEN, QV_ROWS = 0, 1, 2, 3, 8


def _pair_cols(off):
    cols = []
    for p in range(B_HEADS // 2):
        e, o = off + HEAD_DIM * (2 * p), off + HEAD_DIM * (2 * p + 1)
        cols += [np.arange(e, e + HALF), np.arange(o, o + HALF),
                 np.arange(e + HALF, e + HEAD_DIM), np.arange(o + HALF, o + HEAD_DIM)]
    return np.concatenate(cols)


def _key_cols():
    return np.concatenate([np.arange(K_OFF, K_OFF + HALF), np.arange(KI_OFF, KI_OFF + HALF),
                           np.arange(K_OFF + HALF, K_OFF + HEAD_DIM), np.arange(KI_OFF + HALF, KI_OFF + HEAD_DIM)])


def _proj_perm():
    zero = D_IN
    vw = np.concatenate([np.arange(VAL_OFF, VAL_OFF + HEAD_DIM), np.arange(W_OFF, W_OFF + IDX_HEADS),
                         np.full((LANES - HEAD_DIM - IDX_HEADS,), zero)])
    perm = np.concatenate([
        np.arange(U_OFF, U_OFF + A_WIDTH), np.arange(V_OFF, V_OFF + A_WIDTH),
        np.arange(ZA_OFF, ZA_OFF + A_WIDTH), np.arange(ZB_OFF, ZB_OFF + B_WIDTH),
        _pair_cols(Q_OFF), _pair_cols(QI_OFF), _key_cols(), vw])
    assert perm.shape[0] == D_PROJ
    return perm


def _permute_cols(w, perm):
    runs, start = [], 0
    for j in range(1, len(perm) + 1):
        if j < len(perm):
            a, b = int(perm[j - 1]), int(perm[j])
            if (a == D_IN and b == D_IN) or (a != D_IN and b != D_IN and b == a + 1):
                continue
        runs.append((int(perm[start]), j - start))
        start = j
    parts = []
    for first, n in runs:
        if first == D_IN:
            parts.append(jnp.zeros(w.shape[:-1] + (n,), w.dtype))
        else:
            parts.append(w[..., first:first + n])
    return jnp.concatenate(parts, axis=-1)


def _silu(x):
    return x * (1.0 / (1.0 + jnp.exp(-x)))


def _dot_nt(a, b):
    return lax.dot_general(a, b, (((1,), (1,)), ((), ())), preferred_element_type=jnp.float32)


def _rope_kernel(pos_ref, invf_ref, sgn_ref, cos_ref, sin_ref):
    ang = pos_ref[...].astype(jnp.float32) * invf_ref[...]
    cos_ref[...] = jnp.cos(ang)
    sin_ref[...] = jnp.sin(ang) * sgn_ref[...]


def _rope_tables(positions):
    b, s = positions.shape
    ts = min(s, 1024)
    inv_freq = ROPE_THETA ** (-jnp.arange(0, HEAD_DIM, 2, dtype=jnp.float32) / HEAD_DIM)
    invf = jnp.tile(inv_freq, LANES // HALF)[None, :]
    sgn = jnp.concatenate([-jnp.ones((LANES // 2,), jnp.float32), jnp.ones((LANES // 2,), jnp.float32)])[None, :]
    out = jax.ShapeDtypeStruct((b, s, LANES), jnp.float32)
    return pl.pallas_call(
        _rope_kernel,
        out_shape=(out, out),
        grid=(b, s // ts),
        in_specs=[pl.BlockSpec((None, ts, 1), lambda i, j: (i, j, 0)),
                  pl.BlockSpec((1, LANES), lambda i, j: (0, 0)),
                  pl.BlockSpec((1, LANES), lambda i, j: (0, 0))],
        out_specs=(pl.BlockSpec((None, ts, LANES), lambda i, j: (i, j, 0)),
                   pl.BlockSpec((None, ts, LANES), lambda i, j: (i, j, 0))),
        name="rope_tables",
    )(positions.reshape(b, s, 1), invf, sgn)


def _mod_kernel(c_ref, w_ref, b_ref, o_ref):
    cond = _silu(c_ref[...])
    o_ref[...] = jnp.dot(cond, w_ref[...], preferred_element_type=jnp.float32,
                         precision=lax.Precision.HIGHEST) + b_ref[...]


def _modulation(c, w_ada, b_ada):
    depth, d, d3 = w_ada.shape
    b = c.shape[0]
    return pl.pallas_call(
        _mod_kernel,
        out_shape=jax.ShapeDtypeStruct((depth, b, d3), jnp.float32),
        grid=(depth, d3 // d),
        in_specs=[pl.BlockSpec((b, d), lambda l, j: (0, 0)),
                  pl.BlockSpec((None, d, d), lambda l, j: (l, 0, j)),
                  pl.BlockSpec((None, 1, d), lambda l, j: (l, 0, j))],
        out_specs=pl.BlockSpec((None, b, d), lambda l, j: (l, 0, j)),
        compiler_params=pltpu.CompilerParams(vmem_limit_bytes=VMEM_LIMIT_BYTES),
        name="adaln_mod",
    )(c, w_ada, b_ada.reshape(depth, 1, d3))


def _rope128(x, cos, sin_signed):
    return x * cos + pltpu.roll(x, LANES // 2, 1) * sin_signed


def _in_proj_kernel(x_ref, mod_ref, cos_ref, sin_ref, w_ref, gv_ref, bv_ref, ws_ref, bs_ref,
                    ya_ref, gb_ref, qr_ref, qi_ref, kk_ref, ki_ref, vx_ref, wi_ref):
    tm = x_ref.shape[0]
    x = x_ref[...]
    mu = jnp.mean(x, axis=-1, keepdims=True)
    xc = x - mu
    var = jnp.mean(xc * xc, axis=-1, keepdims=True)
    xn = xc * lax.rsqrt(var + LN_EPS)
    h = xn * (1.0 + mod_ref[1]) + mod_ref[0]
    hb = h.astype(jnp.bfloat16)

    def proj(c0, n):
        return jnp.dot(hb, w_ref[:, c0:c0 + n], preferred_element_type=jnp.float32)

    v = proj(C_V, A_WIDTH)
    vmu = jnp.mean(v, axis=-1, keepdims=True)
    vc = v - vmu
    vvar = jnp.mean(vc * vc, axis=-1, keepdims=True)
    vn = (vc * lax.rsqrt(vvar + LN_EPS) * gv_ref[...] + bv_ref[...]).astype(jnp.bfloat16)
    r_i = lax.broadcasted_iota(jnp.int32, (CHUNK, CHUNK), 0)
    c_i = lax.broadcasted_iota(jnp.int32, (CHUNK, CHUNK), 1)
    wm = [jnp.where(r_i >= c_i, ws_ref[g], 0.0).astype(jnp.bfloat16) for g in range(A_GROUPS)]
    gate_a = _silu(proj(C_ZA, A_WIDTH)) * proj(C_U, A_WIDTH)
    for c in range(tm // CHUNK):
        rows = slice(c * CHUNK, (c + 1) * CHUNK)
        mixed = jnp.concatenate(
            [jnp.dot(wm[g], vn[rows, g * A_GDIM:(g + 1) * A_GDIM], preferred_element_type=jnp.float32)
             for g in range(A_GROUPS)], axis=1) + bs_ref[...]
        ya_ref[rows, :] = (gate_a[rows, :] * mixed).astype(jnp.bfloat16)

    gb_ref[...] = _silu(proj(C_ZB, B_WIDTH)).astype(jnp.bfloat16)
    cos = cos_ref[...]
    sin = sin_ref[...]
    q_scale = HEAD_DIM ** -0.5 * math.log2(math.e)
    qi_scale = IDX_DIM ** -0.5
    q = proj(C_Q, B_WIDTH)
    qi = proj(C_QI, IDX_HEADS * IDX_DIM)
    for p in range(B_WIDTH // LANES):
        ls = slice(p * LANES, (p + 1) * LANES)
        qr_ref[:, ls] = (_rope128(q[:, ls], cos, sin) * q_scale).astype(jnp.bfloat16)
        qi_ref[:, ls] = (_rope128(qi[:, ls], cos, sin) * qi_scale).astype(jnp.bfloat16)
    narrow = proj(C_KEYS, 2 * LANES)
    keys = _rope128(narrow[:, :LANES], cos, sin)
    vw = narrow[:, LANES:]
    lane = lax.broadcasted_iota(jnp.int32, vw.shape, 1)
    odd_quarter = (lane // HALF) % 2 == 1
    kk_ref[...] = jnp.where(odd_quarter, pltpu.roll(keys, HALF, 1), keys).astype(jnp.bfloat16)
    ki_ref[...] = jnp.where(odd_quarter, keys, pltpu.roll(keys, LANES - HALF, 1)).astype(jnp.bfloat16)
    vx_ref[...] = jnp.where(lane < HEAD_DIM, vw, 1.0).astype(jnp.bfloat16)
    wi_ref[...] = vw * (IDX_HEADS ** -0.5)


def _in_proj(x, mod4, cos, sin, w_p, layer, gv, bv, ws, bs_full, tm):
    b, s, d = x.shape
    row = lambda i, j: (i, j, 0)
    const2 = lambda i, j: (0, 0)
    bf = jnp.bfloat16
    shapes = (
        jax.ShapeDtypeStruct((b, s, A_WIDTH), bf),
        jax.ShapeDtypeStruct((b, s, B_WIDTH), bf),
        jax.ShapeDtypeStruct((b, s, B_WIDTH), bf),
        jax.ShapeDtypeStruct((b, s, B_WIDTH), bf),
        jax.ShapeDtypeStruct((b, s, LANES), bf),
        jax.ShapeDtypeStruct((b, s, LANES), bf),
        jax.ShapeDtypeStruct((b, s, LANES), bf),
        jax.ShapeDtypeStruct((b, s, LANES), jnp.float32),
    )
    wide = pl.BlockSpec((None, tm, B_WIDTH), row)
    narrow = pl.BlockSpec((None, tm, LANES), row)
    return pl.pallas_call(
        _in_proj_kernel,
        out_shape=shapes,
        grid=(b, s // tm),
        in_specs=[pl.BlockSpec((None, tm, d), row),
                  pl.BlockSpec((None, 3, 1, d), lambda i, j: (i, 0, 0, 0)),
                  narrow, narrow,
                  pl.BlockSpec((None, d, D_PROJ), lambda i, j: (layer, 0, 0)),
                  pl.BlockSpec((1, A_WIDTH), const2),
                  pl.BlockSpec((1, A_WIDTH), const2),
                  pl.BlockSpec((A_GROUPS, CHUNK, CHUNK), lambda i, j: (0, 0, 0)),
                  pl.BlockSpec((CHUNK, A_WIDTH), const2)],
        out_specs=(wide, wide, wide, wide, narrow, narrow, narrow, narrow),
        compiler_params=pltpu.CompilerParams(vmem_limit_bytes=VMEM_LIMIT_BYTES),
        name="in_proj",
    )(x, mod4, cos, sin, w_p, gv, bv, ws, bs_full)


def _key32_to_float(key):
    bits = jnp.where(key >= 0, key, key ^ 0x7FFFFFFF)
    return lax.bitcast_convert_type(bits, jnp.float32)


def _key16_to_float(key):
    bits = jnp.where(key >= 0, key, key ^ 0x7FFF)
    return lax.bitcast_convert_type(jnp.left_shift(bits, 16), jnp.float32)


def _key16_to_key32(key):
    return jnp.left_shift(key, 16) + jnp.where(key < 0, 0xFFFF, 0)


def _attn_kernel(qi_ref, qr_ref, wi_ref, gb_ref, ki_ref, kk_ref, vx_ref, yb_ref,
                 score_ref, sb_ref, rank_ref, qim_ref, qrm_ref, wt_ref, qv_ref, m_ref, acc_ref, *, top_k):
    i = pl.program_id(1)
    tq = Q_TILE
    ck = KEY_CHUNK
    hq = tq // Q_PARTS
    hrows = B_HEADS * hq
    all_parts = tuple(range(Q_PARTS))
    n_full = i
    n_chunks = i + 1
    diag = pl.multiple_of(i * ck, ck)
    f32 = jnp.float32
    bf16 = jnp.bfloat16

    lane = lax.broadcasted_iota(jnp.int32, (hq, LANES), 1)
    even = (lane % HEAD_DIM) < HALF

    def query_positions(halves):
        n_q = len(halves) * hq
        return i * tq + halves[0] * hq + lax.broadcasted_iota(jnp.int32, (1, n_q), 1)

    for half in all_parts:
        qs = slice(half * hq, (half + 1) * hq)
        for p in range(B_HEADS // 2):
            ls = slice(p * LANES, (p + 1) * LANES)
            base = half * hrows + 2 * p * hq
            for src, dst in ((qi_ref, qim_ref), (qr_ref, qrm_ref)):
                pair = src[qs, ls]
                zero = jnp.zeros_like(pair)
                dst[base:base + hq, :] = jnp.where(even, pair, zero)
                dst[base + hq:base + 2 * hq, :] = jnp.where(even, zero, pair)
    wt_ref[...] = wi_ref[...].T

    def chunk_off(c):
        return pl.multiple_of(c * ck, ck)

    def half_rows(halves):
        return slice(halves[0] * hrows, (halves[-1] + 1) * hrows)

    def half_lanes(halves):
        return slice(halves[0] * hq, (halves[-1] + 1) * hq)

    def key_positions(key_off, n_k):
        return key_off + lax.broadcasted_iota(jnp.int32, (n_k, 1), 0)

    def idx_block(key_off, n_k, halves, causal):
        logit_t = _dot_nt(ki_ref[pl.ds(key_off, n_k), :], qim_ref[half_rows(halves), :])
        parts = []
        for j, half in enumerate(halves):
            acc = None
            for h in range(IDX_HEADS):
                col = (j * IDX_HEADS + h) * hq
                w_row = wt_ref[W_LANE + h:W_LANE + h + 1, half * hq:(half + 1) * hq]
                term = w_row * jnp.maximum(logit_t[:, col:col + hq], 0.0)
                acc = term if acc is None else acc + term
            parts.append(acc)
        score = parts[0] if len(parts) == 1 else jnp.concatenate(parts, axis=1)
        lanes = half_lanes(halves)
        if causal:
            score = jnp.where(key_positions(key_off, n_k) <= query_positions(halves), score, NEG_INF)
        score_ref[pl.ds(key_off, n_k), lanes] = score
        sb_ref[pl.ds(key_off, n_k), lanes] = score.astype(bf16)

    def idx_body(c, carry):
        idx_block(chunk_off(c), ck, all_parts, False)
        return carry

    lax.fori_loop(0, n_full, idx_body, 0)
    for part in all_parts:
        n_vis = (part + 1) * hq
        idx_block(diag, n_vis, (part,), True)
        if n_vis < ck:
            dead_off = pl.multiple_of(diag + n_vis, hq)
            lanes = slice(part * hq, (part + 1) * hq)
            score_ref[pl.ds(dead_off, ck - n_vis), lanes] = jnp.full((ck - n_vis, hq), NEG_INF, f32)
            sb_ref[pl.ds(dead_off, ck - n_vis), lanes] = jnp.full((ck - n_vis, hq), NEG_INF, bf16)

    def count_ge_in(ref, thr, one, zero):
        qv_ref[QV_PROBE:QV_PROBE + 1, :] = thr
        dtype = ref.dtype

        def thr_from(part):
            row = qv_ref[QV_PROBE:QV_PROBE + 1, part * hq:]
            return jnp.broadcast_to(row, (COUNT_ROWS, tq - part * hq)).astype(dtype)

        thr_all = thr_from(0)

        def body(c, acc):
            off = chunk_off(c)
            for r in range(0, ck, COUNT_ROWS):
                acc = acc + jnp.where(ref[pl.ds(off + r, COUNT_ROWS), :] >= thr_all, one, zero)
            return acc

        acc = lax.fori_loop(0, n_full, body, jnp.zeros((COUNT_ROWS, tq), dtype))
        done = []
        for g in all_parts:
            thr_g = thr_all if g == 0 else thr_from(g)
            for r in range(g * hq, (g + 1) * hq, COUNT_ROWS):
                acc = acc + jnp.where(ref[pl.ds(diag + r, COUNT_ROWS), g * hq:] >= thr_g, one, zero)
            if g < Q_PARTS - 1:
                done.append(acc[:, :hq])
                acc = acc[:, hq:]
            else:
                done.append(acc)
        acc = jnp.concatenate(done, axis=1)
        return jnp.sum(acc.astype(f32), axis=0, keepdims=True)

    def count_ge(thr):
        return count_ge_in(score_ref, thr, 1.0, 0.0)

    def count_ge_bf16(thr):
        return count_ge_in(sb_ref, thr, jnp.ones((), bf16), jnp.zeros((), bf16))

    kf = float(top_k)
    n_nonneg = count_ge_bf16(jnp.zeros((1, tq), f32))
    n_pos = count_ge_bf16(jnp.full((1, tq), MIN_NORMAL, f32))
    is_pos = n_pos >= kf
    is_neg = n_nonneg < kf
    is_zero = jnp.logical_not(jnp.logical_or(is_pos, is_neg))

    lo16 = jnp.where(is_pos, KEY16_MIN_NORMAL, jnp.where(is_neg, KEY16_NEG_INF, 0)).astype(jnp.int32)
    hi16 = jnp.where(is_pos, KEY16_POS_INF + 1, jnp.where(is_neg, KEY16_NEG_MIN_NORMAL + 1, 1)).astype(jnp.int32)

    def bf16_step(_, carry):
        lo, hi = carry
        mid = lo + jnp.right_shift(hi - lo, 1)
        ge = count_ge_bf16(_key16_to_float(mid)) >= kf
        return jnp.where(ge, mid, lo), jnp.where(ge, hi, mid)

    lo16, hi16 = lax.fori_loop(0, BF16_SEARCH_STEPS, bf16_step, (lo16, hi16))

    lo32 = jnp.maximum(_key16_to_key32(lo16) - 0x8001, KEY32_NEG_INF)
    hi32 = _key16_to_key32(lo16 + 1)
    lo32 = jnp.where(is_pos, jnp.maximum(lo32, KEY32_MIN_NORMAL), lo32)
    hi32 = jnp.where(is_neg, jnp.minimum(hi32, KEY32_NEG_MIN_NORMAL + 1), hi32)
    lo32 = jnp.where(is_zero, 0, lo32)
    hi32 = jnp.where(is_zero, 1, hi32)
    n_lo = jnp.where(is_zero, n_nonneg, COUNT_UNKNOWN)

    def f32_active(lo, hi, n_at_lo):
        return jnp.logical_and(n_at_lo != kf, hi - lo > 1)

    def any_true(mask):
        return jnp.max(jnp.where(mask, 1.0, 0.0)) > 0.0

    def f32_cond(carry):
        step, _, _, _, more = carry
        return jnp.logical_and(more, step < F32_SEARCH_MAX_STEPS)

    def f32_halve(lo, hi, n_at_lo):
        active = f32_active(lo, hi, n_at_lo)
        mid = lo + jnp.right_shift(hi - lo, 1)
        n_mid = count_ge(_key32_to_float(mid))
        up = jnp.logical_and(active, n_mid >= kf)
        down = jnp.logical_and(active, n_mid < kf)
        return jnp.where(up, mid, lo), jnp.where(down, mid, hi), jnp.where(up, n_mid, n_at_lo)

    def f32_step(carry):
        step, lo, hi, n_at_lo, _ = carry
        for _ in range(F32_STEPS_PER_TEST):
            lo, hi, n_at_lo = f32_halve(lo, hi, n_at_lo)
        return step + F32_STEPS_PER_TEST, lo, hi, n_at_lo, any_true(f32_active(lo, hi, n_at_lo))

    _, lo32, hi32, n_lo, _ = lax.while_loop(
        f32_cond, f32_step, (jnp.int32(0), lo32, hi32, n_lo, any_true(f32_active(lo32, hi32, n_lo))))

    thr = jnp.where(is_zero, 0.0, _key32_to_float(lo32))
    thr_next = jnp.where(is_zero, MIN_NORMAL, _key32_to_float(lo32 + 1))
    maybe_tie = n_lo > kf

    def rank_ties():
        r_i = lax.broadcasted_iota(jnp.int32, (CHUNK, CHUNK), 0)
        c_i = lax.broadcasted_iota(jnp.int32, (CHUNK, CHUNK), 1)
        tri = jnp.where(c_i <= r_i, 1.0, 0.0).astype(bf16)

        def body(c, carry):
            above, ties_before = carry
            off = chunk_off(c)
            for r in range(0, ck, CHUNK):
                s = score_ref[pl.ds(off + r, CHUNK), :]
                is_above = jnp.where(s >= thr_next, 1.0, 0.0)
                tie = jnp.where(s >= thr, 1.0 - is_above, 0.0).astype(bf16)
                for r2 in range(0, CHUNK, COUNT_ROWS):
                    above = above + is_above[r2:r2 + COUNT_ROWS]
                rank = jnp.dot(tri, tie, preferred_element_type=f32) + ties_before
                rank_ref[pl.ds(off + r, CHUNK), :] = rank
                ties_before = rank[CHUNK - 1:CHUNK, :]
            return above, ties_before

        above, _ = lax.fori_loop(0, n_chunks, body, (jnp.zeros((COUNT_ROWS, tq), f32), jnp.zeros((1, tq), f32)))
        return kf - jnp.sum(above, axis=0, keepdims=True)

    def no_ties():
        def body(c, carry):
            rank_ref[pl.ds(chunk_off(c), ck), :] = jnp.zeros((ck, tq), f32)
            return carry

        lax.fori_loop(0, n_chunks, body, 0)
        return jnp.full((1, tq), COUNT_UNKNOWN, f32)

    ties_taken = lax.cond(any_true(maybe_tie), rank_ties, no_ties)
    qv_ref[QV_THR:QV_THR + 1, :] = thr
    qv_ref[QV_NEXT:QV_NEXT + 1, :] = thr_next
    qv_ref[QV_TAKEN:QV_TAKEN + 1, :] = ties_taken

    m_ref[...] = jnp.full(m_ref.shape, M_INIT, f32)
    acc_ref[...] = jnp.zeros(acc_ref.shape, f32)

    def att_block(key_off, n_k, halves, causal):
        rows, lanes, nh = half_rows(halves), half_lanes(halves), len(halves)
        s_idx = score_ref[pl.ds(key_off, n_k), lanes]
        thr_q = qv_ref[QV_THR:QV_THR + 1, lanes]
        next_q = qv_ref[QV_NEXT:QV_NEXT + 1, lanes]
        taken_q = qv_ref[QV_TAKEN:QV_TAKEN + 1, lanes]
        keep_tie = jnp.where(rank_ref[pl.ds(key_off, n_k), lanes] <= taken_q, 0.0, NEG_INF)
        bias_t = jnp.where(s_idx >= thr_q, jnp.where(s_idx >= next_q, 0.0, keep_tie), NEG_INF)
        if causal:
            bias_t = jnp.where(key_positions(key_off, n_k) <= query_positions(halves), bias_t, NEG_INF)
        bias = bias_t.T
        s = _dot_nt(qrm_ref[rows, :], kk_ref[pl.ds(key_off, n_k), :])
        s = (s.reshape(nh, B_HEADS, hq, n_k) + bias.reshape(nh, 1, hq, n_k)).reshape(nh * hrows, n_k)
        m_old = m_ref[rows, :]
        m_new = jnp.maximum(m_old, jnp.max(s, axis=1, keepdims=True))
        p = jnp.concatenate(
            [jnp.exp2(s[:, j * LANES:(j + 1) * LANES] - m_new) for j in range(n_k // LANES)],
            axis=1).astype(bf16)
        acc_ref[rows, :] = acc_ref[rows, :] * jnp.exp2(m_old - m_new) + jnp.dot(
            p, vx_ref[pl.ds(key_off, n_k), :], preferred_element_type=f32)
        m_ref[rows, :] = m_new

    def att_body(c, carry):
        for part in all_parts:
            att_block(chunk_off(c), ck, (part,), False)
        return carry

    lax.fori_loop(0, n_full, att_body, 0)
    for part in all_parts:
        att_block(diag, (part + 1) * hq, (part,), True)

    for half in all_parts:
        qs = slice(half * hq, (half + 1) * hq)
        for p in range(B_HEADS // 2):
            base = half * hrows + 2 * p * hq
            a_even = acc_ref[base:base + hq, :]
            a_odd = acc_ref[base + hq:base + 2 * hq, :]
            pair = jnp.where(lane < HEAD_DIM, a_even / pltpu.roll(a_even, HEAD_DIM, 1),
                             pltpu.roll(a_odd, HEAD_DIM, 1) / a_odd)
            ls = slice(p * LANES, (p + 1) * LANES)
            yb_ref[qs, ls] = (pair * gb_ref[qs, ls].astype(f32)).astype(bf16)


def _sparse_attn(qi, qr, wi, gb, ki, kk, vx):
    b, s, _ = qr.shape
    top_k = min(TOPK_MAX, s // 4)
    blk = lambda i, j: (i, j, 0)
    full = lambda i, j: (i, 0, 0)
    wide = pl.BlockSpec((None, Q_TILE, B_WIDTH), blk)
    keys = pl.BlockSpec((None, s, LANES), full)
    return pl.pallas_call(
        functools.partial(_attn_kernel, top_k=top_k),
        out_shape=jax.ShapeDtypeStruct((b, s, B_WIDTH), jnp.bfloat16),
        grid=(b, s // Q_TILE),
        in_specs=[wide, wide, pl.BlockSpec((None, Q_TILE, LANES), blk), wide, keys, keys, keys],
        out_specs=wide,
        scratch_shapes=[pltpu.VMEM((s, Q_TILE), jnp.float32),
                        pltpu.VMEM((s, Q_TILE), jnp.bfloat16),
                        pltpu.VMEM((s, Q_TILE), jnp.float32),
                        pltpu.VMEM((IDX_HEADS * Q_TILE, LANES), jnp.bfloat16),
                        pltpu.VMEM((B_HEADS * Q_TILE, LANES), jnp.bfloat16),
                        pltpu.VMEM((LANES, Q_TILE), jnp.float32),
                        pltpu.VMEM((QV_ROWS, Q_TILE), jnp.float32),
                        pltpu.VMEM((B_HEADS * Q_TILE, LANES), jnp.float32),
                        pltpu.VMEM((B_HEADS * Q_TILE, LANES), jnp.float32)],
        compiler_params=pltpu.CompilerParams(vmem_limit_bytes=VMEM_LIMIT_BYTES),
        name="sparse_attn",
    )(qi, qr, wi, gb, ki, kk, vx)


def _out_proj_kernel(x_ref, ya_ref, yb_ref, w_ref, mod_ref, g_ref, b_ref, o_ref, *, alpha):
    y = (jnp.dot(ya_ref[...], w_ref[:A_WIDTH, :], preferred_element_type=jnp.float32)
         + jnp.dot(yb_ref[...], w_ref[A_WIDTH:, :], preferred_element_type=jnp.float32))
    r = alpha * x_ref[...] + mod_ref[2] * y
    mu = jnp.mean(r, axis=-1, keepdims=True)
    rc = r - mu
    var = jnp.mean(rc * rc, axis=-1, keepdims=True)
    o_ref[...] = rc * lax.rsqrt(var + LN_EPS) * g_ref[...] + b_ref[...]


def _out_proj(x, ya, yb, w_out, layer, mod4, ln_g, ln_b, alpha, tm):
    b, s, d = x.shape
    row = lambda i, j: (i, j, 0)
    const2 = lambda i, j: (0, 0)
    return pl.pallas_call(
        functools.partial(_out_proj_kernel, alpha=alpha),
        out_shape=jax.ShapeDtypeStruct((b, s, d), jnp.float32),
        grid=(b, s // tm),
        in_specs=[pl.BlockSpec((None, tm, d), row),
                  pl.BlockSpec((None, tm, A_WIDTH), row),
                  pl.BlockSpec((None, tm, B_WIDTH), row),
                  pl.BlockSpec((None, A_WIDTH + B_WIDTH, d), lambda i, j: (layer, 0, 0)),
                  pl.BlockSpec((None, 3, 1, d), lambda i, j: (i, 0, 0, 0)),
                  pl.BlockSpec((1, d), const2),
                  pl.BlockSpec((1, d), const2)],
        out_specs=pl.BlockSpec((None, tm, d), row),
        compiler_params=pltpu.CompilerParams(vmem_limit_bytes=VMEM_LIMIT_BYTES),
        name="out_proj",
    )(x, ya, yb, w_out, mod4, ln_g, ln_b)


def kernel(x, c, positions, w_ada, b_ada, w_in, v_norm_g, v_norm_b, w_spatial, b_spatial, w_out, ln_g, ln_b):
    depth, d, _ = w_in.shape
    b, s, _ = x.shape
    assert d == D_MODEL and s % Q_TILE == 0
    tm = min(IN_PROJ_ROWS, s)
    tm_out = min(OUT_PROJ_ROWS, s)
    alpha = (2.0 * depth) ** 0.25

    cos, sin = _rope_tables(positions)
    mod = _modulation(c, w_ada, b_ada)
    mod4 = mod.reshape(depth, b, 3, 1, d)

    w_p = _permute_cols(w_in.astype(jnp.bfloat16), _proj_perm())
    w_o = w_out.astype(jnp.bfloat16)
    bs_full = jnp.repeat(jnp.swapaxes(b_spatial, 1, 2), A_GDIM, axis=2)

    for l in range(depth):
        ya, gb, qr, qi, kk, ki, vx, wi = _in_proj(
            x, mod4[l], cos, sin, w_p, l, v_norm_g[l][None, :], v_norm_b[l][None, :],
            w_spatial[l], bs_full[l], tm)
        yb = _sparse_attn(qi, qr, wi, gb, ki, kk, vx)
        x = _out_proj(x, ya, yb, w_o, l, mod4[l], ln_g[l][None, :], ln_b[l][None, :], alpha, tm_out)
    return x
```

```python
import functools
import math

import numpy as np
import jax
import jax.numpy as jnp
from jax import lax
from jax.experimental import pallas as pl
from jax.experimental.pallas import tpu as pltpu

D_MODEL = 1024
A_WIDTH = D_MODEL // 2
A_GROUPS = 8
A_GDIM = A_WIDTH // A_GROUPS
CHUNK = 128
B_HEADS = 8
HEAD_DIM = 64
HALF = HEAD_DIM // 2
B_WIDTH = B_HEADS * HEAD_DIM
IDX_HEADS = 8
IDX_DIM = HEAD_DIM
TOPK_MAX = 256
ROPE_THETA = 10000.0
LN_EPS = 1e-5

LANES = 128
VMEM_LIMIT_BYTES = 56 * 1024 * 1024

SPLITS = (A_WIDTH, A_WIDTH, A_WIDTH, B_WIDTH, HEAD_DIM, HEAD_DIM, B_WIDTH,
          IDX_HEADS * IDX_DIM, IDX_DIM, IDX_HEADS)
D_IN = sum(SPLITS)
_OFF = np.concatenate([[0], np.cumsum(SPLITS)]).astype(np.int64)
U_OFF, V_OFF, ZA_OFF, Q_OFF, K_OFF, VAL_OFF, ZB_OFF, QI_OFF, KI_OFF, W_OFF = (int(o) for o in _OFF[:-1])

C_U, C_V, C_ZA, C_ZB, C_Q, C_QI = 0, 512, 1024, 1536, 2048, 2560
C_KEYS, C_VW = 3072, 3200
D_PROJ = 3328
W_LANE = HEAD_DIM

IN_PROJ_ROWS = 1024
OUT_PROJ_ROWS = 2048
Q_TILE = 512
KEY_CHUNK = Q_TILE
Q_PARTS = 4
COUNT_ROWS = 64
NEG_INF = float("-inf")
M_INIT = -1e30

MIN_NORMAL = float(np.finfo(np.float32).tiny)
KEY32_NEG_INF = -(2 ** 31) + 0x007FFFFF
KEY32_MIN_NORMAL = 0x00800000
KEY32_NEG_MIN_NORMAL = -0x00800000 - 1
KEY16_NEG_INF = -0x8000 + 0x007F
KEY16_MIN_NORMAL = 0x0080
KEY16_NEG_MIN_NORMAL = -0x0080 - 1
KEY16_POS_INF = 0x7F80
BF16_SEARCH_STEPS = 15
F32_SEARCH_MAX_STEPS = 20
F32_STEPS_PER_TEST = 2
F32_UNTESTED_STEPS = 4
COUNT_UNKNOWN = 1e9
QV_PROBE, QV_THR, QV_NEXT, QV_TAKEN, QV_ROWS = 0, 1, 2, 3, 8


def _pair_cols(off):
    cols = []
    for p in range(B_HEADS // 2):
        e, o = off + HEAD_DIM * (2 * p), off + HEAD_DIM * (2 * p + 1)
        cols += [np.arange(e, e + HALF), np.arange(o, o + HALF),
                 np.arange(e + HALF, e + HEAD_DIM), np.arange(o + HALF, o + HEAD_DIM)]
    return np.concatenate(cols)


def _key_cols():
    return np.concatenate([np.arange(K_OFF, K_OFF + HALF), np.arange(KI_OFF, KI_OFF + HALF),
                           np.arange(K_OFF + HALF, K_OFF + HEAD_DIM), np.arange(KI_OFF + HALF, KI_OFF + HEAD_DIM)])


def _proj_perm():
    zero = D_IN
    vw = np.concatenate([np.arange(VAL_OFF, VAL_OFF + HEAD_DIM), np.arange(W_OFF, W_OFF + IDX_HEADS),
                         np.full((LANES - HEAD_DIM - IDX_HEADS,), zero)])
    perm = np.concatenate([
        np.arange(U_OFF, U_OFF + A_WIDTH), np.arange(V_OFF, V_OFF + A_WIDTH),
        np.arange(ZA_OFF, ZA_OFF + A_WIDTH), np.arange(ZB_OFF, ZB_OFF + B_WIDTH),
        _pair_cols(Q_OFF), _pair_cols(QI_OFF), _key_cols(), vw])
    assert perm.shape[0] == D_PROJ
    return perm


def _permute_cols(w, perm):
    runs, start = [], 0
    for j in range(1, len(perm) + 1):
        if j < len(perm):
            a, b = int(perm[j - 1]), int(perm[j])
            if (a == D_IN and b == D_IN) or (a != D_IN and b != D_IN and b == a + 1):
                continue
        runs.append((int(perm[start]), j - start))
        start = j
    parts = []
    for first, n in runs:
        if first == D_IN:
            parts.append(jnp.zeros(w.shape[:-1] + (n,), w.dtype))
        else:
            parts.append(w[..., first:first + n])
    return jnp.concatenate(parts, axis=-1)


def _silu(x):
    return x * (1.0 / (1.0 + jnp.exp(-x)))


def _dot_nt(a, b):
    return lax.dot_general(a, b, (((1,), (1,)), ((), ())), preferred_element_type=jnp.float32)


def _rope_kernel(pos_ref, invf_ref, sgn_ref, cos_ref, sin_ref):
    ang = pos_ref[...].astype(jnp.float32) * invf_ref[...]
    cos_ref[...] = jnp.cos(ang)
    sin_ref[...] = jnp.sin(ang) * sgn_ref[...]


def _rope_tables(positions):
    b, s = positions.shape
    ts = min(s, 1024)
    inv_freq = ROPE_THETA ** (-jnp.arange(0, HEAD_DIM, 2, dtype=jnp.float32) / HEAD_DIM)
    invf = jnp.tile(inv_freq, LANES // HALF)[None, :]
    sgn = jnp.concatenate([-jnp.ones((LANES // 2,), jnp.float32), jnp.ones((LANES // 2,), jnp.float32)])[None, :]
    out = jax.ShapeDtypeStruct((b, s, LANES), jnp.float32)
    return pl.pallas_call(
        _rope_kernel,
        out_shape=(out, out),
        grid=(b, s // ts),
        in_specs=[pl.BlockSpec((None, ts, 1), lambda i, j: (i, j, 0)),
                  pl.BlockSpec((1, LANES), lambda i, j: (0, 0)),
                  pl.BlockSpec((1, LANES), lambda i, j: (0, 0))],
        out_specs=(pl.BlockSpec((None, ts, LANES), lambda i, j: (i, j, 0)),
                   pl.BlockSpec((None, ts, LANES), lambda i, j: (i, j, 0))),
        name="rope_tables",
    )(positions.reshape(b, s, 1), invf, sgn)


def _mod_kernel(c_ref, w_ref, b_ref, o_ref):
    cond = _silu(c_ref[...])
    o_ref[...] = jnp.dot(cond, w_ref[...], preferred_element_type=jnp.float32,
                         precision=lax.Precision.HIGHEST) + b_ref[...]


def _modulation(c, w_ada, b_ada):
    depth, d, d3 = w_ada.shape
    b = c.shape[0]
    return pl.pallas_call(
        _mod_kernel,
        out_shape=jax.ShapeDtypeStruct((depth, b, d3), jnp.float32),
        grid=(depth, d3 // d),
        in_specs=[pl.BlockSpec((b, d), lambda l, j: (0, 0)),
                  pl.BlockSpec((None, d, d), lambda l, j: (l, 0, j)),
                  pl.BlockSpec((None, 1, d), lambda l, j: (l, 0, j))],
        out_specs=pl.BlockSpec((None, b, d), lambda l, j: (l, 0, j)),
        compiler_params=pltpu.CompilerParams(vmem_limit_bytes=VMEM_LIMIT_BYTES),
        name="adaln_mod",
    )(c, w_ada, b_ada.reshape(depth, 1, d3))


def _rope128(x, cos, sin_signed):
    return x * cos + pltpu.roll(x, LANES // 2, 1) * sin_signed


def _in_proj_kernel(x_ref, mod_ref, cos_ref, sin_ref, w_ref, gv_ref, bv_ref, ws_ref, bs_ref,
                    ya_ref, gb_ref, qr_ref, qi_ref, kk_ref, ki_ref, vx_ref, wi_ref):
    tm = x_ref.shape[0]
    x = x_ref[...]
    mu = jnp.mean(x, axis=-1, keepdims=True)
    xc = x - mu
    var = jnp.mean(xc * xc, axis=-1, keepdims=True)
    xn = xc * lax.rsqrt(var + LN_EPS)
    h = xn * (1.0 + mod_ref[1]) + mod_ref[0]
    hb = h.astype(jnp.bfloat16)

    def proj(c0, n):
        return jnp.dot(hb, w_ref[:, c0:c0 + n], preferred_element_type=jnp.float32)

    v = proj(C_V, A_WIDTH)
    vmu = jnp.mean(v, axis=-1, keepdims=True)
    vc = v - vmu
    vvar = jnp.mean(vc * vc, axis=-1, keepdims=True)
    vn = (vc * lax.rsqrt(vvar + LN_EPS) * gv_ref[...] + bv_ref[...]).astype(jnp.bfloat16)
    r_i = lax.broadcasted_iota(jnp.int32, (CHUNK, CHUNK), 0)
    c_i = lax.broadcasted_iota(jnp.int32, (CHUNK, CHUNK), 1)
    wm = [jnp.where(r_i >= c_i, ws_ref[g], 0.0).astype(jnp.bfloat16) for g in range(A_GROUPS)]
    gate_a = _silu(proj(C_ZA, A_WIDTH)) * proj(C_U, A_WIDTH)
    for c in range(tm // CHUNK):
        rows = slice(c * CHUNK, (c + 1) * CHUNK)
        mixed = jnp.concatenate(
            [jnp.dot(wm[g], vn[rows, g * A_GDIM:(g + 1) * A_GDIM], preferred_element_type=jnp.float32)
             for g in range(A_GROUPS)], axis=1) + bs_ref[...]
        ya_ref[rows, :] = (gate_a[rows, :] * mixed).astype(jnp.bfloat16)

    gb_ref[...] = _silu(proj(C_ZB, B_WIDTH)).astype(jnp.bfloat16)
    cos = cos_ref[...]
    sin = sin_ref[...]
    q_scale = HEAD_DIM ** -0.5 * math.log2(math.e)
    qi_scale = IDX_DIM ** -0.5
    q = proj(C_Q, B_WIDTH)
    qi = proj(C_QI, IDX_HEADS * IDX_DIM)
    for p in range(B_WIDTH // LANES):
        ls = slice(p * LANES, (p + 1) * LANES)
        qr_ref[:, ls] = (_rope128(q[:, ls], cos, sin) * q_scale).astype(jnp.bfloat16)
        qi_ref[:, ls] = (_rope128(qi[:, ls], cos, sin) * qi_scale).astype(jnp.bfloat16)
    narrow = proj(C_KEYS, 2 * LANES)
    keys = _rope128(narrow[:, :LANES], cos, sin)
    vw = narrow[:, LANES:]
    lane = lax.broadcasted_iota(jnp.int32, vw.shape, 1)
    odd_quarter = (lane // HALF) % 2 == 1
    kk_ref[...] = jnp.where(odd_quarter, pltpu.roll(keys, HALF, 1), keys).astype(jnp.bfloat16)
    ki_ref[...] = jnp.where(odd_quarter, keys, pltpu.roll(keys, LANES - HALF, 1)).astype(jnp.bfloat16)
    vx_ref[...] = jnp.where(lane < HEAD_DIM, vw, 1.0).astype(jnp.bfloat16)
    wi_ref[...] = vw * (IDX_HEADS ** -0.5)


def _in_proj(x, mod4, cos, sin, w_p, layer, gv, bv, ws, bs_full, tm):
    b, s, d = x.shape
    row = lambda i, j: (i, j, 0)
    const2 = lambda i, j: (0, 0)
    bf = jnp.bfloat16
    shapes = (
        jax.ShapeDtypeStruct((b, s, A_WIDTH), bf),
        jax.ShapeDtypeStruct((b, s, B_WIDTH), bf),
        jax.ShapeDtypeStruct((b, s, B_WIDTH), bf),
        jax.ShapeDtypeStruct((b, s, B_WIDTH), bf),
        jax.ShapeDtypeStruct((b, s, LANES), bf),
        jax.ShapeDtypeStruct((b, s, LANES), bf),
        jax.ShapeDtypeStruct((b, s, LANES), bf),
        jax.ShapeDtypeStruct((b, s, LANES), jnp.float32),
    )
    wide = pl.BlockSpec((None, tm, B_WIDTH), row)
    narrow = pl.BlockSpec((None, tm, LANES), row)
    return pl.pallas_call(
        _in_proj_kernel,
        out_shape=shapes,
        grid=(b, s // tm),
        in_specs=[pl.BlockSpec((None, tm, d), row),
                  pl.BlockSpec((None, 3, 1, d), lambda i, j: (i, 0, 0, 0)),
                  narrow, narrow,
                  pl.BlockSpec((None, d, D_PROJ), lambda i, j: (layer, 0, 0)),
                  pl.BlockSpec((1, A_WIDTH), const2),
                  pl.BlockSpec((1, A_WIDTH), const2),
                  pl.BlockSpec((A_GROUPS, CHUNK, CHUNK), lambda i, j: (0, 0, 0)),
                  pl.BlockSpec((CHUNK, A_WIDTH), const2)],
        out_specs=(wide, wide, wide, wide, narrow, narrow, narrow, narrow),
        compiler_params=pltpu.CompilerParams(vmem_limit_bytes=VMEM_LIMIT_BYTES),
        name="in_proj",
    )(x, mod4, cos, sin, w_p, gv, bv, ws, bs_full)


def _key32_to_float(key):
    bits = jnp.where(key >= 0, key, key ^ 0x7FFFFFFF)
    return lax.bitcast_convert_type(bits, jnp.float32)


def _key16_to_float(key):
    bits = jnp.where(key >= 0, key, key ^ 0x7FFF)
    return lax.bitcast_convert_type(jnp.left_shift(bits, 16), jnp.float32)


def _key16_to_key32(key):
    return jnp.left_shift(key, 16) + jnp.where(key < 0, 0xFFFF, 0)


def _attn_kernel(qi_ref, qr_ref, wi_ref, gb_ref, ki_ref, kk_ref, vx_ref, yb_ref,
                 score_ref, sb_ref, rank_ref, qim_ref, qrm_ref, wt_ref, qv_ref, m_ref, acc_ref, *, top_k):
    i = pl.program_id(1)
    tq = Q_TILE
    ck = KEY_CHUNK
    hq = tq // Q_PARTS
    hrows = B_HEADS * hq
    all_parts = tuple(range(Q_PARTS))
    n_full = i
    n_chunks = i + 1
    diag = pl.multiple_of(i * ck, ck)
    f32 = jnp.float32
    bf16 = jnp.bfloat16

    lane = lax.broadcasted_iota(jnp.int32, (hq, LANES), 1)
    even = (lane % HEAD_DIM) < HALF

    def query_positions(halves):
        n_q = len(halves) * hq
        return i * tq + halves[0] * hq + lax.broadcasted_iota(jnp.int32, (1, n_q), 1)

    for half in all_parts:
        qs = slice(half * hq, (half + 1) * hq)
        for p in range(B_HEADS // 2):
            ls = slice(p * LANES, (p + 1) * LANES)
            base = half * hrows + 2 * p * hq
            for src, dst in ((qi_ref, qim_ref), (qr_ref, qrm_ref)):
                pair = src[qs, ls]
                zero = jnp.zeros_like(pair)
                dst[base:base + hq, :] = jnp.where(even, pair, zero)
                dst[base + hq:base + 2 * hq, :] = jnp.where(even, zero, pair)
    wt_ref[...] = wi_ref[...].T

    def chunk_off(c):
        return pl.multiple_of(c * ck, ck)

    def half_rows(halves):
        return slice(halves[0] * hrows, (halves[-1] + 1) * hrows)

    def half_lanes(halves):
        return slice(halves[0] * hq, (halves[-1] + 1) * hq)

    def key_positions(key_off, n_k):
        return key_off + lax.broadcasted_iota(jnp.int32, (n_k, 1), 0)

    def idx_block(key_off, n_k, halves, causal):
        logit_t = _dot_nt(ki_ref[pl.ds(key_off, n_k), :], qim_ref[half_rows(halves), :])
        parts = []
        for j, half in enumerate(halves):
            acc = None
            for h in range(IDX_HEADS):
                col = (j * IDX_HEADS + h) * hq
                w_row = wt_ref[W_LANE + h:W_LANE + h + 1, half * hq:(half + 1) * hq]
                term = w_row * jnp.maximum(logit_t[:, col:col + hq], 0.0)
                acc = term if acc is None else acc + term
            parts.append(acc)
        score = parts[0] if len(parts) == 1 else jnp.concatenate(parts, axis=1)
        lanes = half_lanes(halves)
        if causal:
            score = jnp.where(key_positions(key_off, n_k) <= query_positions(halves), score, NEG_INF)
        score_ref[pl.ds(key_off, n_k), lanes] = score
        sb_ref[pl.ds(key_off, n_k), lanes] = score.astype(bf16)

    def idx_body(c, carry):
        idx_block(chunk_off(c), ck, all_parts, False)
        return carry

    lax.fori_loop(0, n_full, idx_body, 0)
    for part in all_parts:
        n_vis = (part + 1) * hq
        idx_block(diag, n_vis, (part,), True)
        if n_vis < ck:
            dead_off = pl.multiple_of(diag + n_vis, hq)
            lanes = slice(part * hq, (part + 1) * hq)
            score_ref[pl.ds(dead_off, ck - n_vis), lanes] = jnp.full((ck - n_vis, hq), NEG_INF, f32)
            sb_ref[pl.ds(dead_off, ck - n_vis), lanes] = jnp.full((ck - n_vis, hq), NEG_INF, bf16)

    def count_ge_in(ref, thr, one, zero):
        qv_ref[QV_PROBE:QV_PROBE + 1, :] = thr
        dtype = ref.dtype

        def thr_from(part):
            row = qv_ref[QV_PROBE:QV_PROBE + 1, part * hq:]
            return jnp.broadcast_to(row, (COUNT_ROWS, tq - part * hq)).astype(dtype)

        thr_all = thr_from(0)

        def body(c, acc):
            off = chunk_off(c)
            for r in range(0, ck, COUNT_ROWS):
                acc = acc + jnp.where(ref[pl.ds(off + r, COUNT_ROWS), :] >= thr_all, one, zero)
            return acc

        acc = lax.fori_loop(0, n_full, body, jnp.zeros((COUNT_ROWS, tq), dtype))
        done = []
        for g in all_parts:
            thr_g = thr_all if g == 0 else thr_from(g)
            for r in range(g * hq, (g + 1) * hq, COUNT_ROWS):
                acc = acc + jnp.where(ref[pl.ds(diag + r, COUNT_ROWS), g * hq:] >= thr_g, one, zero)
            if g < Q_PARTS - 1:
                done.append(acc[:, :hq])
                acc = acc[:, hq:]
            else:
                done.append(acc)
        acc = jnp.concatenate(done, axis=1)
        return jnp.sum(acc.astype(f32), axis=0, keepdims=True)

    def count_ge(thr):
        return count_ge_in(score_ref, thr, 1.0, 0.0)

    def count_ge_bf16(thr):
        return count_ge_in(sb_ref, thr, jnp.ones((), bf16), jnp.zeros((), bf16))

    kf = float(top_k)
    n_nonneg = count_ge_bf16(jnp.zeros((1, tq), f32))
    n_pos = count_ge_bf16(jnp.full((1, tq), MIN_NORMAL, f32))
    is_pos = n_pos >= kf
    is_neg = n_nonneg < kf
    is_zero = jnp.logical_not(jnp.logical_or(is_pos, is_neg))

    lo16 = jnp.where(is_pos, KEY16_MIN_NORMAL, jnp.where(is_neg, KEY16_NEG_INF, 0)).astype(jnp.int32)
    hi16 = jnp.where(is_pos, KEY16_POS_INF + 1, jnp.where(is_neg, KEY16_NEG_MIN_NORMAL + 1, 1)).astype(jnp.int32)

    def bf16_step(_, carry):
        lo, hi = carry
        mid = lo + jnp.right_shift(hi - lo, 1)
        ge = count_ge_bf16(_key16_to_float(mid)) >= kf
        return jnp.where(ge, mid, lo), jnp.where(ge, hi, mid)

    lo16, hi16 = lax.fori_loop(0, BF16_SEARCH_STEPS, bf16_step, (lo16, hi16))

    lo32 = jnp.maximum(_key16_to_key32(lo16) - 0x8001, KEY32_NEG_INF)
    hi32 = _key16_to_key32(lo16 + 1)
    lo32 = jnp.where(is_pos, jnp.maximum(lo32, KEY32_MIN_NORMAL), lo32)
    hi32 = jnp.where(is_neg, jnp.minimum(hi32, KEY32_NEG_MIN_NORMAL + 1), hi32)
    lo32 = jnp.where(is_zero, 0, lo32)
    hi32 = jnp.where(is_zero, 1, hi32)
    n_lo = jnp.where(is_zero, n_nonneg, COUNT_UNKNOWN)

    def f32_active(lo, hi, n_at_lo):
        return jnp.logical_and(n_at_lo != kf, hi - lo > 1)

    def any_true(mask):
        return jnp.max(jnp.where(mask, 1.0, 0.0)) > 0.0

    def f32_cond(carry):
        step, _, _, _, more = carry
        return jnp.logical_and(more, step < F32_SEARCH_MAX_STEPS)

    def f32_halve(lo, hi, n_at_lo):
        active = f32_active(lo, hi, n_at_lo)
        mid = lo + jnp.right_shift(hi - lo, 1)
        n_mid = count_ge(_key32_to_float(mid))
        up = jnp.logical_and(active, n_mid >= kf)
        down = jnp.logical_and(active, n_mid < kf)
        return jnp.where(up, mid, lo), jnp.where(down, mid, hi), jnp.where(up, n_mid, n_at_lo)

    def f32_step(carry):
        step, lo, hi, n_at_lo, _ = carry
        for _ in range(F32_STEPS_PER_TEST):
            lo, hi, n_at_lo = f32_halve(lo, hi, n_at_lo)
        return step + F32_STEPS_PER_TEST, lo, hi, n_at_lo, any_true(f32_active(lo, hi, n_at_lo))

    for _ in range(F32_UNTESTED_STEPS):
        lo32, hi32, n_lo = f32_halve(lo32, hi32, n_lo)
    _, lo32, hi32, n_lo, _ = lax.while_loop(
        f32_cond, f32_step, (jnp.int32(0), lo32, hi32, n_lo, any_true(f32_active(lo32, hi32, n_lo))))

    thr = jnp.where(is_zero, 0.0, _key32_to_float(lo32))
    thr_next = jnp.where(is_zero, MIN_NORMAL, _key32_to_float(lo32 + 1))
    maybe_tie = n_lo > kf

    def rank_ties():
        r_i = lax.broadcasted_iota(jnp.int32, (CHUNK, CHUNK), 0)
        c_i = lax.broadcasted_iota(jnp.int32, (CHUNK, CHUNK), 1)
        tri = jnp.where(c_i <= r_i, 1.0, 0.0).astype(bf16)

        def body(c, carry):
            above, ties_before = carry
            off = chunk_off(c)
            for r in range(0, ck, CHUNK):
                s = score_ref[pl.ds(off + r, CHUNK), :]
                is_above = jnp.where(s >= thr_next, 1.0, 0.0)
                tie = jnp.where(s >= thr, 1.0 - is_above, 0.0).astype(bf16)
                for r2 in range(0, CHUNK, COUNT_ROWS):
                    above = above + is_above[r2:r2 + COUNT_ROWS]
                rank = jnp.dot(tri, tie, preferred_element_type=f32) + ties_before
                rank_ref[pl.ds(off + r, CHUNK), :] = rank
                ties_before = rank[CHUNK - 1:CHUNK, :]
            return above, ties_before

        above, _ = lax.fori_loop(0, n_chunks, body, (jnp.zeros((COUNT_ROWS, tq), f32), jnp.zeros((1, tq), f32)))
        return kf - jnp.sum(above, axis=0, keepdims=True)

    def no_ties():
        def body(c, carry):
            rank_ref[pl.ds(chunk_off(c), ck), :] = jnp.zeros((ck, tq), f32)
            return carry

        lax.fori_loop(0, n_chunks, body, 0)
        return jnp.full((1, tq), COUNT_UNKNOWN, f32)

    ties_taken = lax.cond(any_true(maybe_tie), rank_ties, no_ties)
    qv_ref[QV_THR:QV_THR + 1, :] = thr
    qv_ref[QV_NEXT:QV_NEXT + 1, :] = thr_next
    qv_ref[QV_TAKEN:QV_TAKEN + 1, :] = ties_taken

    m_ref[...] = jnp.full(m_ref.shape, M_INIT, f32)
    acc_ref[...] = jnp.zeros(acc_ref.shape, f32)

    def att_block(key_off, n_k, halves, causal):
        rows, lanes, nh = half_rows(halves), half_lanes(halves), len(halves)
        s_idx = score_ref[pl.ds(key_off, n_k), lanes]
        thr_q = qv_ref[QV_THR:QV_THR + 1, lanes]
        next_q = qv_ref[QV_NEXT:QV_NEXT + 1, lanes]
        taken_q = qv_ref[QV_TAKEN:QV_TAKEN + 1, lanes]
        keep_tie = jnp.where(rank_ref[pl.ds(key_off, n_k), lanes] <= taken_q, 0.0, NEG_INF)
        bias_t = jnp.where(s_idx >= thr_q, jnp.where(s_idx >= next_q, 0.0, keep_tie), NEG_INF)
        if causal:
            bias_t = jnp.where(key_positions(key_off, n_k) <= query_positions(halves), bias_t, NEG_INF)
        bias = bias_t.T
        s = _dot_nt(qrm_ref[rows, :], kk_ref[pl.ds(key_off, n_k), :])
        s = (s.reshape(nh, B_HEADS, hq, n_k) + bias.reshape(nh, 1, hq, n_k)).reshape(nh * hrows, n_k)
        m_old = m_ref[rows, :]
        m_new = jnp.maximum(m_old, jnp.max(s, axis=1, keepdims=True))
        p = jnp.concatenate(
            [jnp.exp2(s[:, j * LANES:(j + 1) * LANES] - m_new) for j in range(n_k // LANES)],
            axis=1).astype(bf16)
        acc_ref[rows, :] = acc_ref[rows, :] * jnp.exp2(m_old - m_new) + jnp.dot(
            p, vx_ref[pl.ds(key_off, n_k), :], preferred_element_type=f32)
        m_ref[rows, :] = m_new

    def att_body(c, carry):
        for part in all_parts:
            att_block(chunk_off(c), ck, (part,), False)
        return carry

    lax.fori_loop(0, n_full, att_body, 0)
    for part in all_parts:
        att_block(diag, (part + 1) * hq, (part,), True)

    for half in all_parts:
        qs = slice(half * hq, (half + 1) * hq)
        for p in range(B_HEADS // 2):
            base = half * hrows + 2 * p * hq
            a_even = acc_ref[base:base + hq, :]
            a_odd = acc_ref[base + hq:base + 2 * hq, :]
            pair = jnp.where(lane < HEAD_DIM, a_even / pltpu.roll(a_even, HEAD_DIM, 1),
                             pltpu.roll(a_odd, HEAD_DIM, 1) / a_odd)
            ls = slice(p * LANES, (p + 1) * LANES)
            yb_ref[qs, ls] = (pair * gb_ref[qs, ls].astype(f32)).astype(bf16)


def _sparse_attn(qi, qr, wi, gb, ki, kk, vx):
    b, s, _ = qr.shape
    top_k = min(TOPK_MAX, s // 4)
    blk = lambda i, j: (i, j, 0)
    full = lambda i, j: (i, 0, 0)
    wide = pl.BlockSpec((None, Q_TILE, B_WIDTH), blk)
    keys = pl.BlockSpec((None, s, LANES), full)
    return pl.pallas_call(
        functools.partial(_attn_kernel, top_k=top_k),
        out_shape=jax.ShapeDtypeStruct((b, s, B_WIDTH), jnp.bfloat16),
        grid=(b, s // Q_TILE),
        in_specs=[wide, wide, pl.BlockSpec((None, Q_TILE, LANES), blk), wide, keys, keys, keys],
        out_specs=wide,
        scratch_shapes=[pltpu.VMEM((s, Q_TILE), jnp.float32),
                        pltpu.VMEM((s, Q_TILE), jnp.bfloat16),
                        pltpu.VMEM((s, Q_TILE), jnp.float32),
                        pltpu.VMEM((IDX_HEADS * Q_TILE, LANES), jnp.bfloat16),
                        pltpu.VMEM((B_HEADS * Q_TILE, LANES), jnp.bfloat16),
                        pltpu.VMEM((LANES, Q_TILE), jnp.float32),
                        pltpu.VMEM((QV_ROWS, Q_TILE), jnp.float32),
                        pltpu.VMEM((B_HEADS * Q_TILE, LANES), jnp.float32),
                        pltpu.VMEM((B_HEADS * Q_TILE, LANES), jnp.float32)],
        compiler_params=pltpu.CompilerParams(vmem_limit_bytes=VMEM_LIMIT_BYTES),
        name="sparse_attn",
    )(qi, qr, wi, gb, ki, kk, vx)


def _out_proj_kernel(x_ref, ya_ref, yb_ref, w_ref, mod_ref, g_ref, b_ref, o_ref, *, alpha):
    y = (jnp.dot(ya_ref[...], w_ref[:A_WIDTH, :], preferred_element_type=jnp.float32)
         + jnp.dot(yb_ref[...], w_ref[A_WIDTH:, :], preferred_element_type=jnp.float32))
    r = alpha * x_ref[...] + mod_ref[2] * y
    mu = jnp.mean(r, axis=-1, keepdims=True)
    rc = r - mu
    var = jnp.mean(rc * rc, axis=-1, keepdims=True)
    o_ref[...] = rc * lax.rsqrt(var + LN_EPS) * g_ref[...] + b_ref[...]


def _out_proj(x, ya, yb, w_out, layer, mod4, ln_g, ln_b, alpha, tm):
    b, s, d = x.shape
    row = lambda i, j: (i, j, 0)
    const2 = lambda i, j: (0, 0)
    return pl.pallas_call(
        functools.partial(_out_proj_kernel, alpha=alpha),
        out_shape=jax.ShapeDtypeStruct((b, s, d), jnp.float32),
        grid=(b, s // tm),
        in_specs=[pl.BlockSpec((None, tm, d), row),
                  pl.BlockSpec((None, tm, A_WIDTH), row),
                  pl.BlockSpec((None, tm, B_WIDTH), row),
                  pl.BlockSpec((None, A_WIDTH + B_WIDTH, d), lambda i, j: (layer, 0, 0)),
                  pl.BlockSpec((None, 3, 1, d), lambda i, j: (i, 0, 0, 0)),
                  pl.BlockSpec((1, d), const2),
                  pl.BlockSpec((1, d), const2)],
        out_specs=pl.BlockSpec((None, tm, d), row),
        compiler_params=pltpu.CompilerParams(vmem_limit_bytes=VMEM_LIMIT_BYTES),
        name="out_proj",
    )(x, ya, yb, w_out, mod4, ln_g, ln_b)


def kernel(x, c, positions, w_ada, b_ada, w_in, v_norm_g, v_norm_b, w_spatial, b_spatial, w_out, ln_g, ln_b):
    depth, d, _ = w_in.shape
    b, s, _ = x.shape
    assert d == D_MODEL and s % Q_TILE == 0
    tm = min(IN_PROJ_ROWS, s)
    tm_out = min(OUT_PROJ_ROWS, s)
    alpha = (2.0 * depth) ** 0.25

    cos, sin = _rope_tables(positions)
    mod = _modulation(c, w_ada, b_ada)
    mod4 = mod.reshape(depth, b, 3, 1, d)

    w_p = _permute_cols(w_in.astype(jnp.bfloat16), _proj_perm())
    w_o = w_out.astype(jnp.bfloat16)
    bs_full = jnp.repeat(jnp.swapaxes(b_spatial, 1, 2), A_GDIM, axis=2)

    for l in range(depth):
        ya, gb, qr, qi, kk, ki, vx, wi = _in_proj(
            x, mod4[l], cos, sin, w_p, l, v_norm_g[l][None, :], v_norm_b[l][None, :],
            w_spatial[l], bs_full[l], tm)
        yb = _sparse_attn(qi, qr, wi, gb, ki, kk, vx)
        x = _out_proj(x, ya, yb, w_o, l, mod4[l], ln_g[l][None, :], ln_b[l][None, :], alpha, tm_out)
    return x
```
